```python
import jax, jax.numpy as jnp
from jax import lax
import numpy as np

D_MODEL = 1024
BATCH = 4
SEQ = 8192
DEPTH = 4
DEC_BATCH = 8
DEC_SEQ = 16
PAST_LEN = 1024

CHUNK = 64
N_MIXERS = 2
N_A = (DEPTH + 1) // 2
N_B = DEPTH // 2
H_A = 4
DK = D_MODEL // (2 * H_A)
DV = D_MODEL // H_A
QK_W = H_A * DK
V_W = H_A * DV
A_PROJ = 2 * QK_W + 2 * V_W + 2 * H_A
GATE_CAP = 15.0
F_BIAS = 3.0
MLP_CHUNK = 128
G_B = 4
MLP_INNER = 2 * D_MODEL
DG = MLP_INNER // G_B
D_FF = -(-8 * D_MODEL // (3 * 256)) * 256
EPS = 1e-6

kernel_name = 'hybrid_mlstm_chunkmlp_stream_step'


def rmsnorm(x, g):
    xf = x.astype(jnp.float32)
    y = xf * lax.rsqrt(jnp.mean(xf * xf, axis=-1, keepdims=True) + EPS)
    return (y * g.astype(jnp.float32)).astype(x.dtype)


def layernorm(x, g):
    xf = x.astype(jnp.float32)
    xc = xf - jnp.mean(xf, axis=-1, keepdims=True)
    y = xc * lax.rsqrt(jnp.mean(xc * xc, axis=-1, keepdims=True) + EPS)
    return (y * g.astype(jnp.float32)).astype(x.dtype)


def soft_cap(x):
    return GATE_CAP * jnp.tanh(x / GATE_CAP)


def mlstm_chunk(carry, inp):
    C, n, m = carry
    q, k, v, ig, lf = inp
    L = q.shape[2]
    b = jnp.cumsum(lf, axis=-1)
    causal = jnp.tril(jnp.ones((L, L), dtype=bool))
    log_d = jnp.where(causal, b[..., :, None] - b[..., None, :] + ig[..., None, :], -jnp.inf)
    log_inter = b + m[..., None]
    m_t = jnp.maximum(log_inter, jnp.max(log_d, axis=-1))
    d = jnp.exp(log_d - m_t[..., None])
    s = jnp.einsum('bhtd,bhsd->bhts', q, k) * d
    w_inter = jnp.exp(log_inter - m_t)
    num = jnp.einsum('bhts,bhsv->bhtv', s, v) + w_inter[..., None] * jnp.einsum('bhvd,bhtd->bhtv', C, q)
    den = jnp.sum(s, axis=-1) + w_inter * jnp.einsum('bhd,bhtd->bht', n, q)
    h = num / jnp.maximum(jnp.abs(den), jnp.exp(-m_t))[..., None]
    m_new = m_t[..., -1]
    decay = jnp.exp(b[..., -1] + m - m_new)
    w_s = jnp.exp(b[..., -1:] - b + ig - m_new[..., None])
    C_new = decay[..., None, None] * C + jnp.einsum('bhs,bhsv,bhsd->bhvd', w_s, v, k)
    n_new = decay[..., None] * n + jnp.einsum('bhs,bhsd->bhd', w_s, k)
    return (C_new, n_new, m_new), h


def mlstm_mixer(xn, state, w_in, b_gate, g_out, w_out):
    B, S, _ = xn.shape
    p = (xn @ w_in).astype(jnp.float32)
    q, k, v, o, gi, gf = jnp.split(
        p, [QK_W, 2 * QK_W, 2 * QK_W + V_W, 2 * QK_W + 2 * V_W, 2 * QK_W + 2 * V_W + H_A], axis=-1)
    gates = soft_cap(jnp.concatenate([gi, gf], axis=-1) + b_gate.astype(jnp.float32))
    ig = gates[..., :H_A]
    lf = jax.nn.log_sigmoid(gates[..., H_A:])
    L = min(CHUNK, S)
    nc = S // L

    def heads(t, dh):
        return t.reshape(B, nc, L, H_A, dh).transpose(1, 0, 3, 2, 4)

    def gate_heads(t):
        return t.reshape(B, nc, L, H_A).transpose(1, 0, 3, 2)

    xs = (heads(q, DK) * (DK ** -0.5), heads(k, DK), heads(v, DV), gate_heads(ig), gate_heads(lf))
    final, h = lax.scan(mlstm_chunk, state, xs)
    h = h.transpose(1, 0, 3, 2, 4).reshape(B, S, H_A, DV)
    h = h * lax.rsqrt(jnp.mean(h * h, axis=-1, keepdims=True) + EPS)
    h = h.reshape(B, S, V_W) * g_out.astype(jnp.float32) * jax.nn.sigmoid(o)
    return h.astype(xn.dtype) @ w_out, final


def chunk_mlp_mixer(xn, w_in, g_v, w_s, b_s, w_out):
    B, S, _ = xn.shape
    uv = jax.nn.gelu(xn @ w_in)
    u, v = jnp.split(uv, 2, axis=-1)
    v = layernorm(v, g_v)
    L = min(MLP_CHUNK, S)
    nc = S // L
    mix = jnp.tril(w_s[:, :L, :L])
    vg = v.reshape(B, nc, L, G_B, DG)
    sv = jnp.einsum('gts,bcsgd->bctgd', mix, vg) + b_s[:, :L].T[None, None, :, :, None]
    out = (u * sv.reshape(B, S, MLP_INNER)) @ w_out
    return out, v


def swiglu(xn, w_gu, w_down):
    g, up = jnp.split(xn @ w_gu, 2, axis=-1)
    return (jax.nn.silu(g) * up) @ w_down


def trunk(x, C0, n0, m0, g_mix, g_ffn, g_final, a_w_in, a_b_gate, a_g_out, a_w_out,
          b_w_in, b_g_v, b_w_s, b_b_s, b_w_out, f_w_gu, f_w_down):
    C_out, n_out, m_out, v_rows = [], [], [], []
    for i in range(DEPTH):
        xn = rmsnorm(x, g_mix[i])
        j = i // N_MIXERS
        if i % N_MIXERS == 0:
            y, (C, n, m) = mlstm_mixer(xn, (C0[:, j], n0[:, j], m0[:, j]),
                                       a_w_in[j], a_b_gate[j], a_g_out[j], a_w_out[j])
            C_out.append(C)
            n_out.append(n)
            m_out.append(m)
        else:
            y, v = chunk_mlp_mixer(xn, b_w_in[j], b_g_v[j], b_w_s[j], b_b_s[j], b_w_out[j])
            v_rows.append(v)
        x = x + y.astype(x.dtype)
        x = x + swiglu(rmsnorm(x, g_ffn[i]), f_w_gu[i], f_w_down[i]).astype(x.dtype)
    y = rmsnorm(x, g_final)
    return y, jnp.stack(C_out, axis=1), jnp.stack(n_out, axis=1), jnp.stack(m_out, axis=1), v_rows


def setup_inputs(seed: int = 0) -> dict:
    key = jax.random.key(seed)
    ks = jax.random.split(key, 24)
    f32 = jnp.float32
    nrm = lambda k, s, sc: jax.random.normal(k, s, f32) * sc
    a_b_gate = jnp.concatenate([
        nrm(ks[8], (N_A, H_A), 0.1),
        F_BIAS + nrm(ks[9], (N_A, H_A), 0.5)], axis=-1)
    return {
        'x_prompt': nrm(ks[0], (BATCH, SEQ, D_MODEL), 1.0),
        'x_sample': nrm(ks[1], (DEC_BATCH, DEC_SEQ, D_MODEL), 1.0),
        'state_C': nrm(ks[2], (DEC_BATCH, N_A, H_A, DV, DK), 0.1),
        'state_n': nrm(ks[3], (DEC_BATCH, N_A, H_A, DK), 0.5),
        'state_m': nrm(ks[4], (DEC_BATCH, N_A, H_A), 0.5),
        'g_mix': 1.0 + nrm(ks[5], (DEPTH, D_MODEL), 0.02),
        'g_ffn': 1.0 + nrm(ks[6], (DEPTH, D_MODEL), 0.02),
        'g_final': 1.0 + nrm(ks[7], (D_MODEL,), 0.02),
        'a_w_in': nrm(ks[10], (N_A, D_MODEL, A_PROJ), D_MODEL ** -0.5),
        'a_b_gate': a_b_gate,
        'a_g_out': 1.0 + nrm(ks[11], (N_A, V_W), 0.02),
        'a_w_out': nrm(ks[12], (N_A, V_W, D_MODEL), V_W ** -0.5),
        'b_w_in': nrm(ks[13], (N_B, D_MODEL, 2 * MLP_INNER), D_MODEL ** -0.5),
        'b_g_v': 1.0 + nrm(ks[14], (N_B, MLP_INNER), 0.02),
        'b_w_s': nrm(ks[15], (N_B, G_B, MLP_CHUNK, MLP_CHUNK), MLP_CHUNK ** -0.5),
        'b_b_s': 1.0 + nrm(ks[16], (N_B, G_B, MLP_CHUNK), 0.02),
        'b_w_out': nrm(ks[17], (N_B, MLP_INNER, D_MODEL), MLP_INNER ** -0.5),
        'f_w_gu': nrm(ks[18], (DEPTH, D_MODEL, 2 * D_FF), D_MODEL ** -0.5),
        'f_w_down': nrm(ks[19], (DEPTH, D_FF, D_MODEL), D_FF ** -0.5),
    }


def reference(x_prompt, x_sample, state_C, state_n, state_m, g_mix, g_ffn, g_final,
              a_w_in, a_b_gate, a_g_out, a_w_out, b_w_in, b_g_v, b_w_s, b_b_s, b_w_out,
              f_w_gu, f_w_down):
    f32 = jnp.float32
    bp = x_prompt.shape[0]
    C0 = jnp.zeros((bp, N_A, H_A, DV, DK), f32)
    n0 = jnp.zeros((bp, N_A, H_A, DK), f32)
    m0 = jnp.zeros((bp, N_A, H_A), f32)
    y_prompt, C_prompt, n_prompt, m_prompt, _ = trunk(
        x_prompt, C0, n0, m0, g_mix, g_ffn, g_final, a_w_in, a_b_gate, a_g_out, a_w_out,
        b_w_in, b_g_v, b_w_s, b_b_s, b_w_out, f_w_gu, f_w_down)
    y_sample, C_sample, n_sample, m_sample, v_list = trunk(
        x_sample, state_C.astype(f32), state_n.astype(f32), state_m.astype(f32),
        g_mix, g_ffn, g_final, a_w_in, a_b_gate, a_g_out, a_w_out,
        b_w_in, b_g_v, b_w_s, b_b_s, b_w_out, f_w_gu, f_w_down)
    v_rows_sample = jnp.stack(v_list, axis=1)
    return (y_prompt, y_sample, C_prompt, n_prompt, m_prompt, C_sample, n_sample, m_sample, v_rows_sample)
```

```python
import functools

import jax
import jax.numpy as jnp
from jax import lax
from jax.experimental import pallas as pl
from jax.experimental.pallas import tpu as pltpu

F32 = jnp.float32
BF16 = jnp.bfloat16

N_HEADS = 4
DK = 128
DV = 256
QK_W = N_HEADS * DK
V_W = N_HEADS * DV
GATE_CAP = 15.0
EPS = 1e-6
MLP_GROUPS = 4
MLP_CHUNK = 128
MLSTM_CHUNK = 64
LANES = 128
GATE_ROWS = 16
VMEM_LIMIT = 56 * 1024 * 1024


def _resident(shape):
    nd = len(shape)
    return pl.BlockSpec(shape, lambda *_: (0,) * nd, pipeline_mode=pl.Buffered(1))


def _params(sem):
    return pltpu.CompilerParams(dimension_semantics=sem, vmem_limit_bytes=VMEM_LIMIT)


def _rms(x, g):
    return x * lax.rsqrt(jnp.mean(x * x, axis=-1, keepdims=True) + EPS) * g


def _dot(a, b):
    return jnp.dot(a, b, preferred_element_type=F32)


def _dot_nt(a, b):
    return lax.dot_general(a, b, (((1,), (1,)), ((), ())), preferred_element_type=F32)


def _split2(x):
    hi = x.astype(BF16)
    lo = (x - hi.astype(F32)).astype(BF16)
    return hi, lo


def _split3(x):
    p1 = x.astype(BF16)
    r = x - p1.astype(F32)
    p2 = r.astype(BF16)
    p3 = (r - p2.astype(F32)).astype(BF16)
    return p1, p2, p3


def _sigmoid(x):
    return 0.5 * (1.0 + jnp.tanh(0.5 * x))


def _log_sigmoid(x):
    return jnp.minimum(x, 0.0) - jnp.log1p(jnp.exp(-jnp.abs(x)))


def _gate_transform(pre, is_input_gate):
    g = GATE_CAP * jnp.tanh(pre / GATE_CAP)
    return jnp.where(is_input_gate, g, _log_sigmoid(g))


def _a_inproj_kernel(x_ref, g_ref, w_ref, wkt_ref, wgc_ref, wgr_ref, bc_ref, br_ref,
                     q_ref, k_ref, kt_ref, v_ref, o_ref, gc_ref, gr_ref):
    xn = _rms(x_ref[...], g_ref[...])
    xh, xl = _split2(xn)
    p = _dot(xh, w_ref[...])
    q_ref[...] = (p[:, :QK_W] * (DK ** -0.5)).astype(BF16)
    k_ref[...] = p[:, QK_W:2 * QK_W].astype(BF16)
    v_ref[...] = p[:, 2 * QK_W:2 * QK_W + V_W].astype(BF16)
    o_ref[...] = p[:, 2 * QK_W + V_W:]
    kt_ref[...] = _dot_nt(wkt_ref[...], xh).astype(BF16)
    wch, wcl = _split2(wgc_ref[...])
    pre_c = _dot(xh, wch) + _dot(xl, wch) + _dot(xh, wcl) + bc_ref[...]
    lane = lax.broadcasted_iota(jnp.int32, pre_c.shape, 1)
    gc_ref[...] = _gate_transform(pre_c, lane < N_HEADS)
    wrh, wrl = _split2(wgr_ref[...])
    pre_r = _dot_nt(wrh, xh) + _dot_nt(wrh, xl) + _dot_nt(wrl, xh) + br_ref[:, :1]
    sub = lax.broadcasted_iota(jnp.int32, pre_r.shape, 0)
    gr_ref[...] = _gate_transform(pre_r, sub < N_HEADS)


def _a_inproj(x, g, w, wkt, wgc, wgr, bc, br, tm):
    m, d = x.shape
    n_p = w.shape[1]
    row = lambda i: (i, 0)
    colb = lambda i: (0, i)
    return pl.pallas_call(
        _a_inproj_kernel,
        grid=(m // tm,),
        in_specs=[pl.BlockSpec((tm, d), row), _resident((1, d)), _resident((d, n_p)),
                  _resident((QK_W, d)), _resident((d, LANES)), _resident((GATE_ROWS, d)),
                  _resident((1, LANES)), _resident((GATE_ROWS, LANES))],
        out_specs=[pl.BlockSpec((tm, QK_W), row), pl.BlockSpec((tm, QK_W), row),
                   pl.BlockSpec((QK_W, tm), colb), pl.BlockSpec((tm, V_W), row),
                   pl.BlockSpec((tm, V_W), row), pl.BlockSpec((tm, LANES), row),
                   pl.BlockSpec((GATE_ROWS, tm), colb)],
        out_shape=[jax.ShapeDtypeStruct((m, QK_W), BF16), jax.ShapeDtypeStruct((m, QK_W), BF16),
                   jax.ShapeDtypeStruct((QK_W, m), BF16), jax.ShapeDtypeStruct((m, V_W), BF16),
                   jax.ShapeDtypeStruct((m, V_W), F32), jax.ShapeDtypeStruct((m, LANES), F32),
                   jax.ShapeDtypeStruct((GATE_ROWS, m), F32)],
        compiler_params=_params(("arbitrary",)),
        name="a_inproj",
    )(x, g, w, wkt, wgc, wgr, bc, br)


def _mlstm_kernel(q_ref, k_ref, kt_ref, v_ref, o_ref, gc_ref, gr_ref, gout_ref, c0_ref, n0_ref, m0_ref,
                  y_ref, c_ref, n_ref, m_ref, ct_s, n_s, m_s, *, valid, cps):
    L = LANES
    step = pl.program_id(1)

    @pl.when(step == 0)
    def _load_state():
        for h in range(N_HEADS):
            ct_s[h] = c0_ref[0, h].T
        n_s[...] = n0_ref[0]
        m_s[...] = m0_ref[0]

    row = lax.broadcasted_iota(jnp.int32, (L, L), 0)
    col = lax.broadcasted_iota(jnp.int32, (L, L), 1)
    causal = col <= row
    if valid < L:
        causal = causal & (col < valid)
    tril = (col <= row).astype(BF16)
    triu = (row <= col).astype(BF16)

    for c in range(cps):
        sl = slice(c * L, (c + 1) * L)
        gc = gc_ref[sl, :]
        gr = gr_ref[:, sl]
        c1, c2, c3 = _split3(gc)
        bcol_all = _dot(tril, c1) + _dot(tril, c2) + _dot(tril, c3)
        r1, r2, r3 = _split3(gr)
        brow_all = _dot(r1, triu) + _dot(r2, triu) + _dot(r3, triu)
        for h in range(N_HEADS):
            ig_col = gc[:, h:h + 1]
            ig_row = gr[h:h + 1, :]
            b_col = bcol_all[:, N_HEADS + h:N_HEADS + h + 1]
            b_row = brow_all[N_HEADS + h:N_HEADS + h + 1, :]
            m_prev = m_s[h:h + 1, 0:1]
            n_row = n_s[h:h + 1, :]
            ct = ct_s[h]
            qh = q_ref[0, sl, h * DK:(h + 1) * DK]
            kh = k_ref[0, sl, h * DK:(h + 1) * DK]
            kth = kt_ref[h * DK:(h + 1) * DK, sl]
            vh = v_ref[0, sl, h * DV:(h + 1) * DV]

            log_inter = b_col + m_prev
            log_d = jnp.where(causal, b_col - b_row + ig_row, -jnp.inf)
            m_t = jnp.maximum(log_inter, jnp.max(log_d, axis=1, keepdims=True))
            s = _dot(qh, kth) * jnp.exp(log_d - m_t)
            w_inter = jnp.exp(log_inter - m_t)
            num = _dot(s.astype(BF16), vh) + w_inter * _dot(qh, ct.astype(BF16))
            den = (jnp.sum(s, axis=1, keepdims=True)
                   + w_inter * jnp.sum(qh.astype(F32) * n_row, axis=1, keepdims=True))
            hh = num * (1.0 / jnp.maximum(jnp.abs(den), jnp.exp(-m_t)))
            hn = hh * lax.rsqrt(jnp.mean(hh * hh, axis=1, keepdims=True) + EPS)
            y = hn * gout_ref[:, h * DV:(h + 1) * DV] * _sigmoid(o_ref[0, sl, h * DV:(h + 1) * DV])
            y_ref[0, sl, h * DV:(h + 1) * DV] = y.astype(BF16)

            m_new = m_t[valid - 1:valid, :]
            b_last = b_col[valid - 1:valid, :]
            decay = jnp.exp(b_last + m_prev - m_new)
            w_s = jnp.exp(b_last - b_col + ig_col - m_new)
            if valid < L:
                w_s = jnp.where(row[:, :1] < valid, w_s, 0.0)
            ct_s[h] = decay * ct + _dot(kth, (w_s * vh.astype(F32)).astype(BF16))
            n_s[h:h + 1, :] = decay * n_row + jnp.sum(w_s * kh.astype(F32), axis=0, keepdims=True)
            m_s[h:h + 1, :] = jnp.broadcast_to(m_new, (1, LANES))

    @pl.when(step == pl.num_programs(1) - 1)
    def _store_state():
        for h in range(N_HEADS):
            c_ref[0, h] = ct_s[h].T
        n_ref[0] = n_s[...]
        m_ref[0] = m_s[...]


def _mlstm(q, k, kt, v, o, gc, gr, gout, c0, n0, m0, valid, cps):
    b, s, _ = q.shape
    tok = cps * LANES
    steps = s // tok
    seq = lambda bi, i: (bi, i, 0)
    flat_row = lambda bi, i: (bi * steps + i, 0)
    flat_col = lambda bi, i: (0, bi * steps + i)
    per_b3 = lambda bi, i: (bi, 0, 0)
    per_b4 = lambda bi, i: (bi, 0, 0, 0)
    return pl.pallas_call(
        functools.partial(_mlstm_kernel, valid=valid, cps=cps),
        grid=(b, steps),
        in_specs=[pl.BlockSpec((1, tok, QK_W), seq), pl.BlockSpec((1, tok, QK_W), seq),
                  pl.BlockSpec((QK_W, tok), flat_col), pl.BlockSpec((1, tok, V_W), seq),
                  pl.BlockSpec((1, tok, V_W), seq), pl.BlockSpec((tok, LANES), flat_row),
                  pl.BlockSpec((GATE_ROWS, tok), flat_col), _resident((1, V_W)),
                  pl.BlockSpec((1, N_HEADS, DV, DK), per_b4), pl.BlockSpec((1, 8, LANES), per_b3),
                  pl.BlockSpec((1, 8, LANES), per_b3)],
        out_specs=[pl.BlockSpec((1, tok, V_W), seq), pl.BlockSpec((1, N_HEADS, DV, DK), per_b4),
                   pl.BlockSpec((1, 8, LANES), per_b3), pl.BlockSpec((1, 8, LANES), per_b3)],
        out_shape=[jax.ShapeDtypeStruct((b, s, V_W), BF16),
                   jax.ShapeDtypeStruct((b, N_HEADS, DV, DK), F32),
                   jax.ShapeDtypeStruct((b, 8, LANES), F32), jax.ShapeDtypeStruct((b, 8, LANES), F32)],
        scratch_shapes=[pltpu.VMEM((N_HEADS, DK, DV), F32), pltpu.VMEM((8, LANES), F32),
                        pltpu.VMEM((8, LANES), F32)],
        compiler_params=_params(("arbitrary", "arbitrary")),
        name="mlstm_scan",
    )(q, k, kt, v, o, gc, gr, gout, c0, n0, m0)


def _ffn_kernel(*refs, has_pre, final_norm, n_chunks):
    refs = list(refs)
    x_ref = refs.pop(0)
    if has_pre:
        y_ref, wo_ref = refs.pop(0), refs.pop(0)
    g_ref, wg_ref, wu_ref, wd_ref = refs[:4]
    refs = refs[4:]
    if final_norm:
        gf_ref = refs.pop(0)
    out_ref, a_s = refs
    x = x_ref[...]
    if has_pre:
        x = x + _dot(y_ref[...], wo_ref[...])
    xb = _rms(x, g_ref[...]).astype(BF16)
    cw = wg_ref.shape[1] // n_chunks
    for c in range(n_chunks):
        cs = slice(c * cw, (c + 1) * cw)
        gate = _dot(xb, wg_ref[:, cs])
        up = _dot(xb, wu_ref[:, cs])
        a_s[:, cs] = (gate * _sigmoid(gate) * up).astype(BF16)
    out = x + _dot(a_s[...], wd_ref[...])
    if final_norm:
        out = _rms(out, gf_ref[...])
    out_ref[...] = out


def _ffn(x, pre, g, wg, wu, wd, g_final, tm, n_chunks):
    m, d = x.shape
    dff = wg.shape[1]
    row = lambda i: (i, 0)
    args, specs = [x], [pl.BlockSpec((tm, d), row)]
    if pre is not None:
        y, wo = pre
        args += [y, wo]
        specs += [pl.BlockSpec((tm, y.shape[1]), row), _resident(wo.shape)]
    args += [g, wg, wu, wd]
    specs += [_resident((1, d)), _resident((d, dff)), _resident((d, dff)), _resident((dff, d))]
    if g_final is not None:
        args.append(g_final)
        specs.append(_resident((1, d)))
    return pl.pallas_call(
        functools.partial(_ffn_kernel, has_pre=pre is not None, final_norm=g_final is not None,
                          n_chunks=n_chunks),
        grid=(m // tm,),
        in_specs=specs,
        out_specs=pl.BlockSpec((tm, d), row),
        out_shape=jax.ShapeDtypeStruct((m, d), F32),
        scratch_shapes=[pltpu.VMEM((tm, dff), BF16)],
        compiler_params=_params(("arbitrary",)),
        name="ffn",
    )(*args)


def _bmix_kernel(x_ref, g_ref, win_ref, gv_ref, mix_ref, bias_ref, wout_ref, *out_refs, chunk_len, emit_v):
    if emit_v:
        out_ref, v_ref, z_s = out_refs
    else:
        out_ref, z_s = out_refs
    L = LANES
    x = x_ref[...]
    tm = x.shape[0]
    inner = gv_ref.shape[1]
    dg = inner // MLP_GROUPS
    xb = _rms(x, g_ref[...]).astype(BF16)
    uv = _dot(xb, win_ref[...])
    uv = 0.5 * uv * (1.0 + jnp.tanh(0.7978845608028654 * (uv + 0.044715 * (uv * uv * uv))))
    u = uv[:, :inner]
    v = uv[:, inner:]
    vc = v - jnp.mean(v, axis=-1, keepdims=True)
    v = vc * lax.rsqrt(jnp.mean(vc * vc, axis=-1, keepdims=True) + EPS) * gv_ref[...]
    if emit_v:
        v_ref[...] = v
    vb = v.astype(BF16)
    row = lax.broadcasted_iota(jnp.int32, (L, L), 0)
    col = lax.broadcasted_iota(jnp.int32, (L, L), 1)
    if chunk_len == L:
        mask = col <= row
    else:
        mask = (row // chunk_len == col // chunk_len) & (col % chunk_len <= row % chunk_len)
    for gi in range(MLP_GROUPS):
        mg = jnp.where(mask, mix_ref[gi], 0.0).astype(BF16)
        bias = bias_ref[:, gi:gi + 1]
        for c in range(tm // L):
            rs = slice(c * L, (c + 1) * L)
            gs = slice(gi * dg, (gi + 1) * dg)
            sv = _dot(mg, vb[rs, gs]) + bias
            z_s[rs, gs] = (u[rs, gs] * sv).astype(BF16)
    out_ref[...] = x + _dot(z_s[...], wout_ref[...])


def _bmix(x, g, win, gv, mix, bias, wout, tm, chunk_len, emit_v):
    m, d = x.shape
    inner = gv.shape[1]
    row = lambda i: (i, 0)
    out_specs = [pl.BlockSpec((tm, d), row)]
    out_shape = [jax.ShapeDtypeStruct((m, d), F32)]
    if emit_v:
        out_specs.append(pl.BlockSpec((tm, inner), row))
        out_shape.append(jax.ShapeDtypeStruct((m, inner), F32))
    return pl.pallas_call(
        functools.partial(_bmix_kernel, chunk_len=chunk_len, emit_v=emit_v),
        grid=(m // tm,),
        in_specs=[pl.BlockSpec((tm, d), row), _resident((1, d)), _resident(win.shape),
                  _resident((1, inner)), _resident(mix.shape), _resident(bias.shape),
                  _resident(wout.shape)],
        out_specs=out_specs,
        out_shape=out_shape,
        scratch_shapes=[pltpu.VMEM((tm, inner), BF16)],
        compiler_params=_params(("arbitrary",)),
        name="chunk_mlp",
    )(x, g, win, gv, mix, bias, wout)


def _prep_weights(g_mix, g_ffn, g_final, a_w_in, a_b_gate, a_g_out, a_w_out,
                  b_w_in, b_g_v, b_w_s, b_b_s, b_w_out, f_w_gu, f_w_down):
    depth = g_mix.shape[0]
    n_proj = 2 * QK_W + 2 * V_W
    n_gate = 2 * N_HEADS
    d_ff = f_w_down.shape[1]
    w = dict(depth=depth, g_final=g_final.reshape(1, -1))
    w["g_mix"] = [g_mix[i].reshape(1, -1) for i in range(depth)]
    w["g_ffn"] = [g_ffn[i].reshape(1, -1) for i in range(depth)]
    w["f_wg"] = [f_w_gu[i, :, :d_ff].astype(BF16) for i in range(depth)]
    w["f_wu"] = [f_w_gu[i, :, d_ff:].astype(BF16) for i in range(depth)]
    w["f_wd"] = [f_w_down[i].astype(BF16) for i in range(depth)]
    a = []
    for j in range(a_w_in.shape[0]):
        wg = a_w_in[j, :, n_proj:]
        a.append(dict(
            w=a_w_in[j, :, :n_proj].astype(BF16),
            wkt=a_w_in[j, :, QK_W:2 * QK_W].T.astype(BF16),
            wgc=jnp.pad(wg, ((0, 0), (0, LANES - n_gate))),
            wgr=jnp.pad(wg.T, ((0, GATE_ROWS - n_gate), (0, 0))),
            bc=jnp.pad(a_b_gate[j], (0, LANES - n_gate)).reshape(1, LANES),
            br=jnp.broadcast_to(jnp.pad(a_b_gate[j], (0, GATE_ROWS - n_gate))[:, None], (GATE_ROWS, LANES)),
            gout=a_g_out[j].reshape(1, -1),
            wout=a_w_out[j].astype(BF16)))
    w["a"] = a
    w["b"] = [dict(win=b_w_in[j].astype(BF16), gv=b_g_v[j].reshape(1, -1), ws=b_w_s[j], bs=b_b_s[j],
                   wout=b_w_out[j].astype(BF16)) for j in range(b_w_in.shape[0])]
    return w


def _pack_state_vec(t):
    b, h, n = t.shape
    t = jnp.broadcast_to(t, (b, h, LANES)) if n == 1 else t
    return jnp.pad(t, ((0, 0), (0, 8 - h), (0, 0)))


def _trunk(x, w, c0, n0, m0, *, seq, tm, cps, ff_chunks):
    bsz, s, d = x.shape
    m = bsz * s
    valid = min(s, LANES)
    mlp_len = min(MLP_CHUNK, s)
    reps = LANES // mlp_len
    xf = x.reshape(m, d)
    c_out, n_out, m_out, v_rows = [], [], [], []
    for i in range(w["depth"]):
        j = i // 2
        pre = None
        if i % 2 == 0:
            a = w["a"][j]
            if valid < LANES:
                xa = jnp.pad(xf.reshape(bsz, s, d), ((0, 0), (0, LANES - s), (0, 0))).reshape(bsz * LANES, d)
                sp = LANES
            else:
                xa, sp = xf, s
            q, k, kt, v, o, gc, gr = _a_inproj(xa, w["g_mix"][i], a["w"], a["wkt"], a["wgc"], a["wgr"],
                                               a["bc"], a["br"], min(tm, xa.shape[0]))
            y, c_new, n_new, m_new = _mlstm(
                q.reshape(bsz, sp, QK_W), k.reshape(bsz, sp, QK_W), kt, v.reshape(bsz, sp, V_W),
                o.reshape(bsz, sp, V_W), gc, gr, a["gout"], c0[:, j],
                _pack_state_vec(n0[:, j]), _pack_state_vec(m0[:, j][..., None]), valid,
                cps if valid == LANES else 1)
            c_out.append(c_new)
            n_out.append(n_new[:, :N_HEADS, :])
            m_out.append(m_new[:, :N_HEADS, 0])
            pre = (y[:, :s].reshape(m, V_W), a["wout"])
        else:
            bw = w["b"][j]
            mix = jnp.tile(bw["ws"][:, :mlp_len, :mlp_len], (1, reps, reps))
            bias = jnp.pad(jnp.tile(bw["bs"][:, :mlp_len], (1, reps)).T, ((0, 0), (0, LANES - MLP_GROUPS)))
            emit_v = s < MLP_CHUNK
            res = _bmix(xf, w["g_mix"][i], bw["win"], bw["gv"], mix, bias, bw["wout"],
                        min(tm, m), mlp_len, emit_v)
            xf = res[0]
            if emit_v:
                v_rows.append(res[1].reshape(bsz, s, -1))
        g_final = w["g_final"] if i == w["depth"] - 1 else None
        xf = _ffn(xf, pre, w["g_ffn"][i], w["f_wg"][i], w["f_wu"][i], w["f_wd"][i], g_final,
                  min(tm, m), ff_chunks)
    return (xf.reshape(bsz, s, d), jnp.stack(c_out, axis=1), jnp.stack(n_out, axis=1),
            jnp.stack(m_out, axis=1), v_rows)


def kernel(x_prompt, x_sample, state_C, state_n, state_m, g_mix, g_ffn, g_final, a_w_in, a_b_gate, a_g_out,
           a_w_out, b_w_in, b_g_v, b_w_s, b_b_s, b_w_out, f_w_gu, f_w_down):
    w = _prep_weights(g_mix, g_ffn, g_final, a_w_in, a_b_gate, a_g_out, a_w_out,
                      b_w_in, b_g_v, b_w_s, b_b_s, b_w_out, f_w_gu, f_w_down)
    bp = x_prompt.shape[0]
    n_a = state_C.shape[1]
    zc = jnp.zeros((bp, n_a, N_HEADS, DV, DK), F32)
    zn = jnp.zeros((bp, n_a, N_HEADS, DK), F32)
    zm = jnp.zeros((bp, n_a, N_HEADS), F32)
    y_p, c_p, n_p, m_p, _ = _trunk(x_prompt, w, zc, zn, zm, seq=x_prompt.shape[1], tm=512, cps=2, ff_chunks=2)
    y_s, c_s, n_s, m_s, v_list = _trunk(x_sample, w, state_C.astype(F32), state_n.astype(F32),
                                        state_m.astype(F32), seq=x_sample.shape[1], tm=512, cps=1,
                                        ff_chunks=2)
    v_rows = jnp.stack(v_list, axis=1)
    return (y_p, y_s, c_p, n_p, m_p, c_s, n_s, m_s, v_rows)
```

```python
import functools

import jax
import jax.numpy as jnp
from jax import lax
from jax.experimental import pallas as pl
from jax.experimental.pallas import tpu as pltpu

F32 = jnp.float32
BF16 = jnp.bfloat16

N_HEADS = 4
DK = 128
DV = 256
QK_W = N_HEADS * DK
V_W = N_HEADS * DV
GATE_CAP = 15.0
EPS = 1e-6
MLP_GROUPS = 4
MLP_CHUNK = 128
LANES = 128
GATE_ROWS = 16
VMEM_LIMIT = 56 * 1024 * 1024


def _resident(shape):
    nd = len(shape)
    return pl.BlockSpec(shape, lambda *_: (0,) * nd, pipeline_mode=pl.Buffered(1))


def _params(sem):
    return pltpu.CompilerParams(dimension_semantics=sem, vmem_limit_bytes=VMEM_LIMIT)


def _rms(x, g):
    return x * lax.rsqrt(jnp.mean(x * x, axis=-1, keepdims=True) + EPS) * g


def _dot(a, b):
    return jnp.dot(a, b, preferred_element_type=F32)


def _dot_nt(a, b):
    return lax.dot_general(a, b, (((1,), (1,)), ((), ())), preferred_element_type=F32)


def _split2(x):
    hi = x.astype(BF16)
    lo = (x - hi.astype(F32)).astype(BF16)
    return hi, lo


def _split3(x):
    p1 = x.astype(BF16)
    r = x - p1.astype(F32)
    p2 = r.astype(BF16)
    p3 = (r - p2.astype(F32)).astype(BF16)
    return p1, p2, p3


def _sigmoid(x):
    return 0.5 * (1.0 + jnp.tanh(0.5 * x))


def _log_sigmoid(x):
    return jnp.minimum(x, 0.0) - jnp.log1p(jnp.exp(-jnp.abs(x)))


def _gate_transform(pre, is_input_gate):
    g = GATE_CAP * jnp.tanh(pre / GATE_CAP)
    return jnp.where(is_input_gate, g, _log_sigmoid(g))


def _a_inproj_kernel(x_ref, g_ref, w_ref, wst_ref, br_ref, q_ref, kt_ref, v_ref, o_ref, gc_ref, gr_ref):
    L = LANES
    xn = _rms(x_ref[...], g_ref[...])
    xh, xl = _split2(xn)
    p = _dot(xh, w_ref[...])
    q_ref[...] = (p[:, :QK_W] * (DK ** -0.5)).astype(BF16)
    v_ref[...] = p[:, QK_W:QK_W + V_W].astype(BF16)
    o_ref[...] = p[:, QK_W + V_W:]
    r_all = _dot_nt(wst_ref[...], xh)
    kt_ref[0] = r_all[:QK_W].astype(BF16)
    pre = (r_all[QK_W:QK_W + GATE_ROWS] + r_all[QK_W + GATE_ROWS:]
           + _dot_nt(wst_ref[QK_W:QK_W + GATE_ROWS, :], xl) + br_ref[:, :1])
    sub = lax.broadcasted_iota(jnp.int32, pre.shape, 0)
    gates = _gate_transform(pre, sub < 8)
    r_i = lax.broadcasted_iota(jnp.int32, (L, L), 0)
    c_i = lax.broadcasted_iota(jnp.int32, (L, L), 1)
    causal = c_i <= r_i
    eye = c_i == r_i
    triu = (r_i <= c_i).astype(BF16)
    for c in range(x_ref.shape[0] // L):
        sl = slice(c * L, (c + 1) * L)
        gch = gates[:, sl]
        r1, r2, r3 = _split3(gch)
        cum = _dot(r1, triu) + _dot(r2, triu) + _dot(r3, triu)
        b_r = cum[8:16]
        a_r = gch[0:8] - b_r
        gr_ref[0, :, sl] = a_r
        out = jnp.zeros((L, L), F32)
        for h in range(N_HEADS):
            a_b = jnp.broadcast_to(a_r[h:h + 1, :], (L, L))
            b_b = jnp.broadcast_to(b_r[h:h + 1, :], (L, L))
            cm_col = jnp.max(jnp.where(causal, a_b, -jnp.inf), axis=1, keepdims=True)
            a_col = jnp.sum(jnp.where(eye, a_b, 0.0), axis=1, keepdims=True)
            b_col = jnp.sum(jnp.where(eye, b_b, 0.0), axis=1, keepdims=True)
            out = jnp.where(c_i == N_HEADS + h, b_col, out)
            out = jnp.where(c_i == 2 * N_HEADS + h, cm_col, out)
            out = jnp.where(c_i == 3 * N_HEADS + h, a_col, out)
        gc_ref[sl, :] = out


def _a_inproj(x, g, w, wst, br, tm, seq):
    m, d = x.shape
    n_p = w.shape[1]
    tpb = seq // tm
    row = lambda i: (i, 0)
    colb = lambda i: (i // tpb, 0, i % tpb)
    return pl.pallas_call(
        _a_inproj_kernel,
        grid=(m // tm,),
        in_specs=[pl.BlockSpec((tm, d), row), _resident((1, d)), _resident((d, n_p)),
                  _resident(wst.shape), _resident((GATE_ROWS, LANES))],
        out_specs=[pl.BlockSpec((tm, QK_W), row), pl.BlockSpec((1, QK_W, tm), colb),
                   pl.BlockSpec((tm, V_W), row), pl.BlockSpec((tm, V_W), row),
                   pl.BlockSpec((tm, LANES), row), pl.BlockSpec((1, 8, tm), colb)],
        out_shape=[jax.ShapeDtypeStruct((m, QK_W), BF16), jax.ShapeDtypeStruct((m // seq, QK_W, seq), BF16),
                   jax.ShapeDtypeStruct((m, V_W), BF16), jax.ShapeDtypeStruct((m, V_W), F32),
                   jax.ShapeDtypeStruct((m, LANES), F32), jax.ShapeDtypeStruct((m // seq, 8, seq), F32)],
        compiler_params=_params(("arbitrary",)),
        name="a_inproj",
    )(x, g, w, wst, br)


def _mlstm_kernel(q_ref, kt_ref, v_ref, o_ref, gc_ref, gr_ref, gout_ref, c0_ref, n0_ref, m0_ref,
                  y_ref, c_ref, n_ref, m_ref, ct_s, n_s, m_s, *, valid):
    L = LANES
    nb = q_ref.shape[0]
    step = pl.program_id(1)

    @pl.when(step == 0)
    def _load_state():
        for bb in range(nb):
            for h in range(N_HEADS):
                si = bb * N_HEADS + h
                ct_s[si] = c0_ref[bb, h].T
                n_s[si] = jnp.broadcast_to(n0_ref[bb, h:h + 1, :], (L, DK)).T
                m_s[si] = jnp.broadcast_to(m0_ref[bb, h:h + 1, :], (8, LANES))

    row = lax.broadcasted_iota(jnp.int32, (L, L), 0)
    col = lax.broadcasted_iota(jnp.int32, (L, L), 1)
    causal = col <= row
    ones = jnp.ones((L, LANES), BF16)

    def rowsum(x):
        hi, lo = _split2(x)
        return _dot(hi, ones) + _dot(lo, ones)

    for bb in range(nb):
        gc = gc_ref[bb]
        gr = gr_ref[bb]
        for h in range(N_HEADS):
            si = bb * N_HEADS + h
            b_rep = jnp.broadcast_to(gc[:, N_HEADS + h:N_HEADS + h + 1], (L, LANES))
            cm_rep = jnp.broadcast_to(gc[:, 2 * N_HEADS + h:2 * N_HEADS + h + 1], (L, LANES))
            a_rep = jnp.broadcast_to(gc[:, 3 * N_HEADS + h:3 * N_HEADS + h + 1], (L, LANES))
            a_row = gr[h:h + 1, :]
            m_prev = m_s[si, 0:1, :]
            ct = ct_s[si]
            n_rep = n_s[si]
            qh = q_ref[bb, :, h * DK:(h + 1) * DK]
            kth = kt_ref[bb, h * DK:(h + 1) * DK, :]
            v0 = v_ref[bb, :, h * DV:h * DV + LANES]
            v1 = v_ref[bb, :, h * DV + LANES:(h + 1) * DV]

            mx = jnp.maximum(m_prev, cm_rep)
            s = _dot(qh, kth) * jnp.where(causal, jnp.exp(a_row - mx), 0.0)
            sb = s.astype(BF16)
            w_inter = jnp.exp(m_prev - mx)
            ctb = ct.astype(BF16)
            den = _dot(sb, ones) + w_inter * _dot(qh, n_rep.astype(BF16))
            m_t = b_rep + mx
            r = 1.0 / jnp.maximum(jnp.abs(den), jnp.exp(-m_t))
            h0 = (_dot(sb, v0) + w_inter * _dot(qh, ctb[:, :LANES])) * r
            h1 = (_dot(sb, v1) + w_inter * _dot(qh, ctb[:, LANES:])) * r
            rs = lax.rsqrt(rowsum(h0 * h0 + h1 * h1) * (1.0 / DV) + EPS)
            c0 = h * DV
            y0 = h0 * rs * gout_ref[:, c0:c0 + LANES] * _sigmoid(o_ref[bb, :, c0:c0 + LANES])
            y1 = h1 * rs * gout_ref[:, c0 + LANES:c0 + DV] * _sigmoid(o_ref[bb, :, c0 + LANES:c0 + DV])
            y_ref[bb, :, c0:c0 + LANES] = y0.astype(BF16)
            y_ref[bb, :, c0 + LANES:c0 + DV] = y1.astype(BF16)

            m_new = m_t[valid - 1:valid, :]
            b_last = b_rep[valid - 1:valid, :]
            decay = jnp.exp(b_last + m_prev - m_new)
            w_s = jnp.exp(a_rep + (b_last - m_new))
            if valid < L:
                w_s = jnp.where(row < valid, w_s, 0.0)
            wv = jnp.concatenate([(w_s * v0.astype(F32)).astype(BF16), (w_s * v1.astype(F32)).astype(BF16)],
                                 axis=1)
            ct_s[si] = jnp.concatenate([decay, decay], axis=1) * ct + _dot(kth, wv)
            n_s[si] = decay * n_rep + _dot(kth, w_s.astype(BF16))
            m_s[si] = jnp.broadcast_to(m_new, (8, LANES))

    @pl.when(step == pl.num_programs(1) - 1)
    def _store_state():
        for bb in range(nb):
            for h in range(N_HEADS):
                si = bb * N_HEADS + h
                c_ref[bb, h] = ct_s[si].T
                n_ref[bb, h:h + 1, :] = n_s[si].T[0:1, :]
                m_ref[bb, h:h + 1, :] = m_s[si, 0:1, :]
        n_ref[:, N_HEADS:, :] = jnp.zeros((nb, 8 - N_HEADS, LANES), F32)
        m_ref[:, N_HEADS:, :] = jnp.zeros((nb, 8 - N_HEADS, LANES), F32)


def _mlstm(q, kt, v, o, gc, gr, gout, c0, n0, m0, valid, nb):
    b, s, _ = q.shape
    steps = s // LANES
    seq = lambda bi, i: (bi, i, 0)
    seq_t = lambda bi, i: (bi, 0, i)
    per_b3 = lambda bi, i: (bi, 0, 0)
    per_b4 = lambda bi, i: (bi, 0, 0, 0)
    return pl.pallas_call(
        functools.partial(_mlstm_kernel, valid=valid),
        grid=(b // nb, steps),
        in_specs=[pl.BlockSpec((nb, LANES, QK_W), seq), pl.BlockSpec((nb, QK_W, LANES), seq_t),
                  pl.BlockSpec((nb, LANES, V_W), seq), pl.BlockSpec((nb, LANES, V_W), seq),
                  pl.BlockSpec((nb, LANES, LANES), seq), pl.BlockSpec((nb, 8, LANES), seq_t),
                  _resident((1, V_W)),
                  pl.BlockSpec((nb, N_HEADS, DV, DK), per_b4), pl.BlockSpec((nb, 8, LANES), per_b3),
                  pl.BlockSpec((nb, 8, LANES), per_b3)],
        out_specs=[pl.BlockSpec((nb, LANES, V_W), seq), pl.BlockSpec((nb, N_HEADS, DV, DK), per_b4),
                   pl.BlockSpec((nb, 8, LANES), per_b3), pl.BlockSpec((nb, 8, LANES), per_b3)],
        out_shape=[jax.ShapeDtypeStruct((b, s, V_W), BF16),
                   jax.ShapeDtypeStruct((b, N_HEADS, DV, DK), F32),
                   jax.ShapeDtypeStruct((b, 8, LANES), F32), jax.ShapeDtypeStruct((b, 8, LANES), F32)],
        scratch_shapes=[pltpu.VMEM((nb * N_HEADS, DK, DV), F32), pltpu.VMEM((nb * N_HEADS, DK, LANES), F32),
                        pltpu.VMEM((nb * N_HEADS, 8, LANES), F32)],
        compiler_params=_params(("arbitrary", "arbitrary")),
        name="mlstm_scan",
    )(q, kt, v, o, gc, gr, gout, c0, n0, m0)


def _ffn_kernel(*refs, has_pre, final_norm, n_chunks):
    refs = list(refs)
    x_ref = refs.pop(0)
    if has_pre:
        y_ref, wo_ref = refs.pop(0), refs.pop(0)
    g_ref, wg_ref, wu_ref, wd_ref = refs[:4]
    refs = refs[4:]
    if final_norm:
        gf_ref = refs.pop(0)
    out_ref, a_s = refs
    x = x_ref[...]
    if has_pre:
        x = x + _dot(y_ref[...], wo_ref[...])
    xb = _rms(x, g_ref[...]).astype(BF16)
    cw = wg_ref.shape[1] // n_chunks
    for c in range(n_chunks):
        cs = slice(c * cw, (c + 1) * cw)
        gate = _dot(xb, wg_ref[:, cs])
        up = _dot(xb, wu_ref[:, cs])
        a_s[:, cs] = (gate * _sigmoid(gate) * up).astype(BF16)
    out = x + _dot(a_s[...], wd_ref[...])
    if final_norm:
        out = _rms(out, gf_ref[...])
    out_ref[...] = out


def _ffn(x, pre, g, wg, wu, wd, g_final, tm, n_chunks):
    m, d = x.shape
    dff = wg.shape[1]
    row = lambda i: (i, 0)
    args, specs = [x], [pl.BlockSpec((tm, d), row)]
    if pre is not None:
        y, wo = pre
        args += [y, wo]
        specs += [pl.BlockSpec((tm, y.shape[1]), row), _resident(wo.shape)]
    args += [g, wg, wu, wd]
    specs += [_resident((1, d)), _resident((d, dff)), _resident((d, dff)), _resident((dff, d))]
    if g_final is not None:
        args.append(g_final)
        specs.append(_resident((1, d)))
    return pl.pallas_call(
        functools.partial(_ffn_kernel, has_pre=pre is not None, final_norm=g_final is not None,
                          n_chunks=n_chunks),
        grid=(m // tm,),
        in_specs=specs,
        out_specs=pl.BlockSpec((tm, d), row),
        out_shape=jax.ShapeDtypeStruct((m, d), F32),
        scratch_shapes=[pltpu.VMEM((tm, dff), BF16)],
        compiler_params=_params(("arbitrary",)),
        name="ffn",
    )(*args)


def _bmix_kernel(x_ref, g_ref, win_ref, gv_ref, mix_ref, bias_ref, wout_ref, *out_refs, chunk_len, emit_v):
    if emit_v:
        out_ref, v_ref, z_s = out_refs
    else:
        out_ref, z_s = out_refs
    L = LANES
    x = x_ref[...]
    tm = x.shape[0]
    inner = gv_ref.shape[1]
    dg = inner // MLP_GROUPS
    xb = _rms(x, g_ref[...]).astype(BF16)
    uv = _dot(xb, win_ref[...])
    uv = 0.5 * uv * (1.0 + jnp.tanh(0.7978845608028654 * (uv + 0.044715 * (uv * uv * uv))))
    u = uv[:, :inner]
    v = uv[:, inner:]
    vc = v - jnp.mean(v, axis=-1, keepdims=True)
    v = vc * lax.rsqrt(jnp.mean(vc * vc, axis=-1, keepdims=True) + EPS) * gv_ref[...]
    if emit_v:
        v_ref[...] = v
    vb = v.astype(BF16)
    row = lax.broadcasted_iota(jnp.int32, (L, L), 0)
    col = lax.broadcasted_iota(jnp.int32, (L, L), 1)
    if chunk_len == L:
        mask = col <= row
    else:
        mask = (row // chunk_len == col // chunk_len) & (col % chunk_len <= row % chunk_len)
    for gi in range(MLP_GROUPS):
        mg = jnp.where(mask, mix_ref[gi], 0.0).astype(BF16)
        bias = bias_ref[:, gi:gi + 1]
        for c in range(tm // L):
            rs = slice(c * L, (c + 1) * L)
            gs = slice(gi * dg, (gi + 1) * dg)
            sv = _dot(mg, vb[rs, gs]) + bias
            z_s[rs, gs] = (u[rs, gs] * sv).astype(BF16)
    out_ref[...] = x + _dot(z_s[...], wout_ref[...])


def _bmix(x, g, win, gv, mix, bias, wout, tm, chunk_len, emit_v):
    m, d = x.shape
    inner = gv.shape[1]
    row = lambda i: (i, 0)
    out_specs = [pl.BlockSpec((tm, d), row)]
    out_shape = [jax.ShapeDtypeStruct((m, d), F32)]
    if emit_v:
        out_specs.append(pl.BlockSpec((tm, inner), row))
        out_shape.append(jax.ShapeDtypeStruct((m, inner), F32))
    return pl.pallas_call(
        functools.partial(_bmix_kernel, chunk_len=chunk_len, emit_v=emit_v),
        grid=(m // tm,),
        in_specs=[pl.BlockSpec((tm, d), row), _resident((1, d)), _resident(win.shape),
                  _resident((1, inner)), _resident(mix.shape), _resident(bias.shape),
                  _resident(wout.shape)],
        out_specs=out_specs,
        out_shape=out_shape,
        scratch_shapes=[pltpu.VMEM((tm, inner), BF16)],
        compiler_params=_params(("arbitrary",)),
        name="chunk_mlp",
    )(x, g, win, gv, mix, bias, wout)


def _gate_rows(t):
    z = jnp.zeros((8 - N_HEADS, t.shape[1]), t.dtype)
    return jnp.concatenate([t[:N_HEADS], z, t[N_HEADS:], z], axis=0)


def _prep_weights(g_mix, g_ffn, g_final, a_w_in, a_b_gate, a_g_out, a_w_out,
                  b_w_in, b_g_v, b_w_s, b_b_s, b_w_out, f_w_gu, f_w_down):
    depth = g_mix.shape[0]
    n_proj = 2 * QK_W + 2 * V_W
    d_ff = f_w_down.shape[1]
    w = dict(depth=depth, g_final=g_final.reshape(1, -1))
    w["g_mix"] = [g_mix[i].reshape(1, -1) for i in range(depth)]
    w["g_ffn"] = [g_ffn[i].reshape(1, -1) for i in range(depth)]
    w["f_wg"] = [f_w_gu[i, :, :d_ff].astype(BF16) for i in range(depth)]
    w["f_wu"] = [f_w_gu[i, :, d_ff:].astype(BF16) for i in range(depth)]
    w["f_wd"] = [f_w_down[i].astype(BF16) for i in range(depth)]
    a = []
    for j in range(a_w_in.shape[0]):
        wk_t = a_w_in[j, :, QK_W:2 * QK_W].T.astype(BF16)
        wg_hi, wg_lo = _split2(_gate_rows(a_w_in[j, :, n_proj:].T))
        a.append(dict(
            w=jnp.concatenate([a_w_in[j, :, :QK_W], a_w_in[j, :, 2 * QK_W:n_proj]], axis=1).astype(BF16),
            wst=jnp.concatenate([wk_t, wg_hi, wg_lo], axis=0),
            br=jnp.broadcast_to(_gate_rows(a_b_gate[j][:, None]), (GATE_ROWS, LANES)),
            gout=a_g_out[j].reshape(1, -1),
            wout=a_w_out[j].astype(BF16)))
    w["a"] = a
    w["b"] = [dict(win=b_w_in[j].astype(BF16), gv=b_g_v[j].reshape(1, -1), ws=b_w_s[j], bs=b_b_s[j],
                   wout=b_w_out[j].astype(BF16)) for j in range(b_w_in.shape[0])]
    return w


def _pack_state_vec(t):
    b, h, n = t.shape
    t = jnp.broadcast_to(t, (b, h, LANES)) if n == 1 else t
    return jnp.pad(t, ((0, 0), (0, 8 - h), (0, 0)))


def _trunk(x, w, c0, n0, m0, *, tm, nb, ff_chunks):
    bsz, s, d = x.shape
    m = bsz * s
    valid = min(s, LANES)
    mlp_len = min(MLP_CHUNK, s)
    reps = LANES // mlp_len
    xf = x.reshape(m, d)
    c_out, n_out, m_out, v_rows = [], [], [], []
    for i in range(w["depth"]):
        j = i // 2
        pre = None
        if i % 2 == 0:
            a = w["a"][j]
            if valid < LANES:
                xa = jnp.pad(xf.reshape(bsz, s, d), ((0, 0), (0, LANES - s), (0, 0))).reshape(bsz * LANES, d)
                sp = LANES
            else:
                xa, sp = xf, s
            q, kt, v, o, gc, gr = _a_inproj(xa, w["g_mix"][i], a["w"], a["wst"], a["br"], min(tm, sp), sp)
            y, c_new, n_new, m_new = _mlstm(
                q.reshape(bsz, sp, QK_W), kt, v.reshape(bsz, sp, V_W), o.reshape(bsz, sp, V_W),
                gc.reshape(bsz, sp, LANES), gr, a["gout"], c0[:, j],
                _pack_state_vec(n0[:, j]), _pack_state_vec(m0[:, j][..., None]), valid, nb)
            c_out.append(c_new)
            n_out.append(n_new[:, :N_HEADS, :])
            m_out.append(m_new[:, :N_HEADS, 0])
            pre = (y[:, :s].reshape(m, V_W), a["wout"])
        else:
            bw = w["b"][j]
            mix = jnp.tile(bw["ws"][:, :mlp_len, :mlp_len], (1, reps, reps))
            bias = jnp.pad(jnp.tile(bw["bs"][:, :mlp_len], (1, reps)).T, ((0, 0), (0, LANES - MLP_GROUPS)))
            emit_v = s < MLP_CHUNK
            res = _bmix(xf, w["g_mix"][i], bw["win"], bw["gv"], mix, bias, bw["wout"],
                        min(tm, m), mlp_len, emit_v)
            xf = res[0]
            if emit_v:
                v_rows.append(res[1].reshape(bsz, s, -1))
        g_final = w["g_final"] if i == w["depth"] - 1 else None
        xf = _ffn(xf, pre, w["g_ffn"][i], w["f_wg"][i], w["f_wu"][i], w["f_wd"][i], g_final,
                  min(tm, m), ff_chunks)
    return (xf.reshape(bsz, s, d), jnp.stack(c_out, axis=1), jnp.stack(n_out, axis=1),
            jnp.stack(m_out, axis=1), v_rows)


def kernel(x_prompt, x_sample, state_C, state_n, state_m, g_mix, g_ffn, g_final, a_w_in, a_b_gate, a_g_out,
           a_w_out, b_w_in, b_g_v, b_w_s, b_b_s, b_w_out, f_w_gu, f_w_down):
    w = _prep_weights(g_mix, g_ffn, g_final, a_w_in, a_b_gate, a_g_out, a_w_out,
                      b_w_in, b_g_v, b_w_s, b_b_s, b_w_out, f_w_gu, f_w_down)
    bp = x_prompt.shape[0]
    n_a = state_C.shape[1]
    zc = jnp.zeros((bp, n_a, N_HEADS, DV, DK), F32)
    zn = jnp.zeros((bp, n_a, N_HEADS, DK), F32)
    zm = jnp.zeros((bp, n_a, N_HEADS), F32)
    y_p, c_p, n_p, m_p, _ = _trunk(x_prompt, w, zc, zn, zm, tm=512, nb=4, ff_chunks=2)
    y_s, c_s, n_s, m_s, v_list = _trunk(x_sample, w, state_C.astype(F32), state_n.astype(F32),
                                        state_m.astype(F32), tm=512, nb=4, ff_chunks=2)
    v_rows = jnp.stack(v_list, axis=1)
    return (y_p, y_s, c_p, n_p, m_p, c_s, n_s, m_s, v_rows)
```

```python
import functools

import jax
import jax.numpy as jnp
from jax import lax
from jax.experimental import pallas as pl
from jax.experimental.pallas import tpu as pltpu

F32 = jnp.float32
BF16 = jnp.bfloat16

N_HEADS = 4
DK = 128
DV = 256
QK_W = N_HEADS * DK
V_W = N_HEADS * DV
GATE_CAP = 15.0
EPS = 1e-6
MLP_GROUPS = 4
MLP_CHUNK = 128
LANES = 128
GATE_ROWS = 16
VMEM_LIMIT = 56 * 1024 * 1024


def _resident(shape):
    nd = len(shape)
    return pl.BlockSpec(shape, lambda *_: (0,) * nd, pipeline_mode=pl.Buffered(1))


def _params(sem):
    return pltpu.CompilerParams(dimension_semantics=sem, vmem_limit_bytes=VMEM_LIMIT)


def _rms(x, g):
    return x * lax.rsqrt(jnp.mean(x * x, axis=-1, keepdims=True) + EPS) * g


def _dot(a, b):
    return jnp.dot(a, b, preferred_element_type=F32)


def _dot_nt(a, b):
    return lax.dot_general(a, b, (((1,), (1,)), ((), ())), preferred_element_type=F32)


def _split2(x):
    hi = x.astype(BF16)
    lo = (x - hi.astype(F32)).astype(BF16)
    return hi, lo


def _split3(x):
    p1 = x.astype(BF16)
    r = x - p1.astype(F32)
    p2 = r.astype(BF16)
    p3 = (r - p2.astype(F32)).astype(BF16)
    return p1, p2, p3


def _sigmoid(x):
    return 0.5 * (1.0 + jnp.tanh(0.5 * x))


def _log_sigmoid(x):
    return jnp.minimum(x, 0.0) - jnp.log1p(jnp.exp(-jnp.abs(x)))


def _gate_transform(pre, is_input_gate):
    g = GATE_CAP * jnp.tanh(pre / GATE_CAP)
    return jnp.where(is_input_gate, g, _log_sigmoid(g))


def _a_inproj_kernel(x_ref, g_ref, w_ref, wst_ref, br_ref, q_ref, kt_ref, v_ref, o_ref, gc_ref, gr_ref):
    L = LANES
    xn = _rms(x_ref[...], g_ref[...])
    xh, xl = _split2(xn)
    r_all = _dot_nt(wst_ref[...], xh)
    kt_ref[0] = r_all[:QK_W].astype(BF16)
    pre = (r_all[QK_W:QK_W + GATE_ROWS] + r_all[QK_W + GATE_ROWS:]
           + _dot_nt(wst_ref[QK_W:QK_W + GATE_ROWS, :], xl) + br_ref[:, :1])
    sub = lax.broadcasted_iota(jnp.int32, pre.shape, 0)
    gates = _gate_transform(pre, sub < 8)
    r_i = lax.broadcasted_iota(jnp.int32, (L, L), 0)
    c_i = lax.broadcasted_iota(jnp.int32, (L, L), 1)
    causal = c_i <= r_i
    eye = c_i == r_i
    triu = (r_i <= c_i).astype(BF16)
    for c in range(x_ref.shape[0] // L):
        sl = slice(c * L, (c + 1) * L)
        gch = gates[:, sl]
        r1, r2, r3 = _split3(gch)
        cum = _dot(r1, triu) + _dot(r2, triu) + _dot(r3, triu)
        b_r = cum[8:16]
        a_r = gch[0:8] - b_r
        gr_ref[0, :, sl] = a_r
        out = jnp.zeros((L, L), F32)
        for h in range(N_HEADS):
            a_b = jnp.broadcast_to(a_r[h:h + 1, :], (L, L))
            b_b = jnp.broadcast_to(b_r[h:h + 1, :], (L, L))
            cm_col = jnp.max(jnp.where(causal, a_b, -jnp.inf), axis=1, keepdims=True)
            a_col = jnp.sum(jnp.where(eye, a_b, 0.0), axis=1, keepdims=True)
            b_col = jnp.sum(jnp.where(eye, b_b, 0.0), axis=1, keepdims=True)
            out = jnp.where(c_i == N_HEADS + h, b_col, out)
            out = jnp.where(c_i == 2 * N_HEADS + h, cm_col, out)
            out = jnp.where(c_i == 3 * N_HEADS + h, a_col, out)
        gc_ref[sl, :] = out
    p = _dot(xh, w_ref[...])
    q_ref[...] = (p[:, :QK_W] * (DK ** -0.5)).astype(BF16)
    v_ref[...] = p[:, QK_W:QK_W + V_W].astype(BF16)
    o_ref[...] = p[:, QK_W + V_W:]


def _a_inproj(x, g, w, wst, br, tm, seq):
    m, d = x.shape
    n_p = w.shape[1]
    tpb = seq // tm
    row = lambda i: (i, 0)
    colb = lambda i: (i // tpb, 0, i % tpb)
    return pl.pallas_call(
        _a_inproj_kernel,
        grid=(m // tm,),
        in_specs=[pl.BlockSpec((tm, d), row), _resident((1, d)), _resident((d, n_p)),
                  _resident(wst.shape), _resident((GATE_ROWS, LANES))],
        out_specs=[pl.BlockSpec((tm, QK_W), row), pl.BlockSpec((1, QK_W, tm), colb),
                   pl.BlockSpec((tm, V_W), row), pl.BlockSpec((tm, V_W), row),
                   pl.BlockSpec((tm, LANES), row), pl.BlockSpec((1, 8, tm), colb)],
        out_shape=[jax.ShapeDtypeStruct((m, QK_W), BF16), jax.ShapeDtypeStruct((m // seq, QK_W, seq), BF16),
                   jax.ShapeDtypeStruct((m, V_W), BF16), jax.ShapeDtypeStruct((m, V_W), F32),
                   jax.ShapeDtypeStruct((m, LANES), F32), jax.ShapeDtypeStruct((m // seq, 8, seq), F32)],
        compiler_params=_params(("arbitrary",)),
        name="a_inproj",
    )(x, g, w, wst, br)


def _mlstm_kernel(q_ref, kt_ref, v_ref, gc_ref, gr_ref, c0_ref, n0_ref, m0_ref,
                  h_ref, c_ref, n_ref, m_ref, ct_s, n_s, m_s, *, valid):
    L = LANES
    nb = q_ref.shape[0]
    step = pl.program_id(1)

    @pl.when(step == 0)
    def _load_state():
        for bb in range(nb):
            for h in range(N_HEADS):
                si = bb * N_HEADS + h
                ct_s[si] = c0_ref[bb, h].T
                n_s[si] = jnp.broadcast_to(n0_ref[bb, h:h + 1, :], (L, DK)).T
                m_s[si] = jnp.broadcast_to(m0_ref[bb, h:h + 1, :], (8, LANES))

    row = lax.broadcasted_iota(jnp.int32, (L, L), 0)
    col = lax.broadcasted_iota(jnp.int32, (L, L), 1)
    causal = col <= row
    ones = jnp.ones((L, LANES), BF16)

    for bb in range(nb):
        gc = gc_ref[bb]
        gr = gr_ref[bb]
        for h in range(N_HEADS):
            si = bb * N_HEADS + h
            b_rep = jnp.broadcast_to(gc[:, N_HEADS + h:N_HEADS + h + 1], (L, LANES))
            cm_rep = jnp.broadcast_to(gc[:, 2 * N_HEADS + h:2 * N_HEADS + h + 1], (L, LANES))
            a_rep = jnp.broadcast_to(gc[:, 3 * N_HEADS + h:3 * N_HEADS + h + 1], (L, LANES))
            a_row = gr[h:h + 1, :]
            m_prev = m_s[si, 0:1, :]
            ct = ct_s[si]
            n_rep = n_s[si]
            qh = q_ref[bb, :, h * DK:(h + 1) * DK]
            kth = kt_ref[bb, h * DK:(h + 1) * DK, :]
            v0 = v_ref[bb, :, h * DV:h * DV + LANES]
            v1 = v_ref[bb, :, h * DV + LANES:(h + 1) * DV]

            mx = jnp.maximum(m_prev, cm_rep)
            s = _dot(qh, kth) * jnp.where(causal, jnp.exp(a_row - mx), 0.0)
            sb = s.astype(BF16)
            w_inter = jnp.exp(m_prev - mx)
            ctb = ct.astype(BF16)
            den = _dot(sb, ones) + w_inter * _dot(qh, n_rep.astype(BF16))
            m_t = b_rep + mx
            r = 1.0 / jnp.maximum(jnp.abs(den), jnp.exp(-m_t))
            h_ref[bb, :, h * DV:h * DV + LANES] = (_dot(sb, v0) + w_inter * _dot(qh, ctb[:, :LANES])) * r
            h_ref[bb, :, h * DV + LANES:(h + 1) * DV] = (_dot(sb, v1) + w_inter * _dot(qh, ctb[:, LANES:])) * r

            m_new = m_t[valid - 1:valid, :]
            b_last = b_rep[valid - 1:valid, :]
            decay = jnp.exp(b_last + m_prev - m_new)
            w_s = jnp.exp(a_rep + (b_last - m_new))
            if valid < L:
                w_s = jnp.where(row < valid, w_s, 0.0)
            wv = jnp.concatenate([(w_s * v0.astype(F32)).astype(BF16), (w_s * v1.astype(F32)).astype(BF16)],
                                 axis=1)
            ct_s[si] = jnp.concatenate([decay, decay], axis=1) * ct + _dot(kth, wv)
            n_s[si] = decay * n_rep + _dot(kth, w_s.astype(BF16))
            m_s[si] = jnp.broadcast_to(m_new, (8, LANES))

    @pl.when(step == pl.num_programs(1) - 1)
    def _store_state():
        for bb in range(nb):
            for h in range(N_HEADS):
                si = bb * N_HEADS + h
                c_ref[bb, h] = ct_s[si].T
                n_ref[bb, h:h + 1, :] = n_s[si].T[0:1, :]
                m_ref[bb, h:h + 1, :] = m_s[si, 0:1, :]
        n_ref[:, N_HEADS:, :] = jnp.zeros((nb, 8 - N_HEADS, LANES), F32)
        m_ref[:, N_HEADS:, :] = jnp.zeros((nb, 8 - N_HEADS, LANES), F32)


def _mlstm(q, kt, v, gc, gr, c0, n0, m0, valid, nb):
    b, s, _ = q.shape
    steps = s // LANES
    seq = lambda bi, i: (bi, i, 0)
    seq_t = lambda bi, i: (bi, 0, i)
    per_b3 = lambda bi, i: (bi, 0, 0)
    per_b4 = lambda bi, i: (bi, 0, 0, 0)
    return pl.pallas_call(
        functools.partial(_mlstm_kernel, valid=valid),
        grid=(b // nb, steps),
        in_specs=[pl.BlockSpec((nb, LANES, QK_W), seq), pl.BlockSpec((nb, QK_W, LANES), seq_t),
                  pl.BlockSpec((nb, LANES, V_W), seq),
                  pl.BlockSpec((nb, LANES, LANES), seq), pl.BlockSpec((nb, 8, LANES), seq_t),
                  pl.BlockSpec((nb, N_HEADS, DV, DK), per_b4), pl.BlockSpec((nb, 8, LANES), per_b3),
                  pl.BlockSpec((nb, 8, LANES), per_b3)],
        out_specs=[pl.BlockSpec((nb, LANES, V_W), seq), pl.BlockSpec((nb, N_HEADS, DV, DK), per_b4),
                   pl.BlockSpec((nb, 8, LANES), per_b3), pl.BlockSpec((nb, 8, LANES), per_b3)],
        out_shape=[jax.ShapeDtypeStruct((b, s, V_W), F32),
                   jax.ShapeDtypeStruct((b, N_HEADS, DV, DK), F32),
                   jax.ShapeDtypeStruct((b, 8, LANES), F32), jax.ShapeDtypeStruct((b, 8, LANES), F32)],
        scratch_shapes=[pltpu.VMEM((nb * N_HEADS, DK, DV), F32), pltpu.VMEM((nb * N_HEADS, DK, LANES), F32),
                        pltpu.VMEM((nb * N_HEADS, 8, LANES), F32)],
        compiler_params=_params(("arbitrary", "arbitrary")),
        name="mlstm_scan",
    )(q, kt, v, gc, gr, c0, n0, m0)


def _ffn_kernel(*refs, has_pre, final_norm, n_chunks):
    refs = list(refs)
    x_ref = refs.pop(0)
    if has_pre:
        h_ref, o_ref, go_ref, wo_ref = refs[:4]
        refs = refs[4:]
    g_ref, wg_ref, wu_ref, wd_ref = refs[:4]
    refs = refs[4:]
    if final_norm:
        gf_ref = refs.pop(0)
    out_ref, a_s = refs[:2]
    x = x_ref[...]
    if has_pre:
        y_s = refs[2]
        for hd in range(N_HEADS):
            cs = slice(hd * DV, (hd + 1) * DV)
            hh = h_ref[:, cs]
            hn = hh * lax.rsqrt(jnp.mean(hh * hh, axis=1, keepdims=True) + EPS)
            y_s[:, cs] = (hn * go_ref[:, cs] * _sigmoid(o_ref[:, cs])).astype(BF16)
        x = x + _dot(y_s[...], wo_ref[...])
    xb = _rms(x, g_ref[...]).astype(BF16)
    cw = wg_ref.shape[1] // n_chunks
    for c in range(n_chunks):
        cs = slice(c * cw, (c + 1) * cw)
        gate = _dot(xb, wg_ref[:, cs])
        up = _dot(xb, wu_ref[:, cs])
        a_s[:, cs] = (gate * _sigmoid(gate) * up).astype(BF16)
    out = x + _dot(a_s[...], wd_ref[...])
    if final_norm:
        out = _rms(out, gf_ref[...])
    out_ref[...] = out


def _ffn(x, pre, g, wg, wu, wd, g_final, tm, n_chunks):
    m, d = x.shape
    dff = wg.shape[1]
    row = lambda i: (i, 0)
    args, specs = [x], [pl.BlockSpec((tm, d), row)]
    scratch = [pltpu.VMEM((tm, dff), BF16)]
    if pre is not None:
        hh, o, gout, wo = pre
        args += [hh, o, gout, wo]
        specs += [pl.BlockSpec((tm, V_W), row), pl.BlockSpec((tm, V_W), row), _resident((1, V_W)),
                  _resident(wo.shape)]
        scratch.append(pltpu.VMEM((tm, V_W), BF16))
    args += [g, wg, wu, wd]
    specs += [_resident((1, d)), _resident((d, dff)), _resident((d, dff)), _resident((dff, d))]
    if g_final is not None:
        args.append(g_final)
        specs.append(_resident((1, d)))
    return pl.pallas_call(
        functools.partial(_ffn_kernel, has_pre=pre is not None, final_norm=g_final is not None,
                          n_chunks=n_chunks),
        grid=(m // tm,),
        in_specs=specs,
        out_specs=pl.BlockSpec((tm, d), row),
        out_shape=jax.ShapeDtypeStruct((m, d), F32),
        scratch_shapes=scratch,
        compiler_params=_params(("arbitrary",)),
        name="ffn",
    )(*args)


def _bmix_kernel(x_ref, g_ref, win_ref, gv_ref, mix_ref, bias_ref, wout_ref, *out_refs, chunk_len, emit_v,
                 sub):
    if emit_v:
        out_ref, v_ref, z_s = out_refs
    else:
        out_ref, z_s = out_refs
    L = LANES
    tm = x_ref.shape[0]
    inner = gv_ref.shape[1]
    dg = inner // MLP_GROUPS

    def gelu(t):
        return 0.5 * t * (1.0 + jnp.tanh(0.7978845608028654 * (t + 0.044715 * (t * t * t))))

    row = lax.broadcasted_iota(jnp.int32, (L, L), 0)
    col = lax.broadcasted_iota(jnp.int32, (L, L), 1)
    if chunk_len == L:
        mask = col <= row
    else:
        mask = (row // chunk_len == col // chunk_len) & (col % chunk_len <= row % chunk_len)
    mgs = [jnp.where(mask, mix_ref[gi], 0.0).astype(BF16) for gi in range(MLP_GROUPS)]

    for t0 in range(0, tm, sub):
        ts = slice(t0, t0 + sub)
        x = x_ref[ts, :]
        xb = _rms(x, g_ref[...]).astype(BF16)
        v = gelu(_dot(xb, win_ref[:, inner:]))
        u = gelu(_dot(xb, win_ref[:, :inner]))
        vc = v - jnp.mean(v, axis=-1, keepdims=True)
        v = vc * lax.rsqrt(jnp.mean(vc * vc, axis=-1, keepdims=True) + EPS) * gv_ref[...]
        if emit_v:
            v_ref[ts, :] = v
        vb = v.astype(BF16)
        for gi in range(MLP_GROUPS):
            bias = bias_ref[:, gi:gi + 1]
            gs = slice(gi * dg, (gi + 1) * dg)
            for c in range(sub // L):
                rs = slice(c * L, (c + 1) * L)
                sv = _dot(mgs[gi], vb[rs, gs]) + bias
                z_s[t0 + c * L:t0 + (c + 1) * L, gs] = (u[rs, gs] * sv).astype(BF16)
        out_ref[ts, :] = x + _dot(z_s[ts, :], wout_ref[...])


def _bmix(x, g, win, gv, mix, bias, wout, tm, chunk_len, emit_v, sub):
    m, d = x.shape
    inner = gv.shape[1]
    row = lambda i: (i, 0)
    out_specs = [pl.BlockSpec((tm, d), row)]
    out_shape = [jax.ShapeDtypeStruct((m, d), F32)]
    if emit_v:
        out_specs.append(pl.BlockSpec((tm, inner), row))
        out_shape.append(jax.ShapeDtypeStruct((m, inner), F32))
    return pl.pallas_call(
        functools.partial(_bmix_kernel, chunk_len=chunk_len, emit_v=emit_v, sub=min(sub, tm)),
        grid=(m // tm,),
        in_specs=[pl.BlockSpec((tm, d), row), _resident((1, d)), _resident(win.shape),
                  _resident((1, inner)), _resident(mix.shape), _resident(bias.shape),
                  _resident(wout.shape)],
        out_specs=out_specs,
        out_shape=out_shape,
        scratch_shapes=[pltpu.VMEM((tm, inner), BF16)],
        compiler_params=_params(("arbitrary",)),
        name="chunk_mlp",
    )(x, g, win, gv, mix, bias, wout)


def _gate_rows(t):
    z = jnp.zeros((8 - N_HEADS, t.shape[1]), t.dtype)
    return jnp.concatenate([t[:N_HEADS], z, t[N_HEADS:], z], axis=0)


def _prep_weights(g_mix, g_ffn, g_final, a_w_in, a_b_gate, a_g_out, a_w_out,
                  b_w_in, b_g_v, b_w_s, b_b_s, b_w_out, f_w_gu, f_w_down):
    depth = g_mix.shape[0]
    n_proj = 2 * QK_W + 2 * V_W
    d_ff = f_w_down.shape[1]
    w = dict(depth=depth, g_final=g_final.reshape(1, -1))
    w["g_mix"] = [g_mix[i].reshape(1, -1) for i in range(depth)]
    w["g_ffn"] = [g_ffn[i].reshape(1, -1) for i in range(depth)]
    w["f_wg"] = [f_w_gu[i, :, :d_ff].astype(BF16) for i in range(depth)]
    w["f_wu"] = [f_w_gu[i, :, d_ff:].astype(BF16) for i in range(depth)]
    w["f_wd"] = [f_w_down[i].astype(BF16) for i in range(depth)]
    a = []
    for j in range(a_w_in.shape[0]):
        wk_t = a_w_in[j, :, QK_W:2 * QK_W].T.astype(BF16)
        wg_hi, wg_lo = _split2(_gate_rows(a_w_in[j, :, n_proj:].T))
        a.append(dict(
            w=jnp.concatenate([a_w_in[j, :, :QK_W], a_w_in[j, :, 2 * QK_W:n_proj]], axis=1).astype(BF16),
            wst=jnp.concatenate([wk_t, wg_hi, wg_lo], axis=0),
            br=jnp.broadcast_to(_gate_rows(a_b_gate[j][:, None]), (GATE_ROWS, LANES)),
            gout=a_g_out[j].reshape(1, -1),
            wout=a_w_out[j].astype(BF16)))
    w["a"] = a
    w["b"] = [dict(win=b_w_in[j].astype(BF16), gv=b_g_v[j].reshape(1, -1), ws=b_w_s[j], bs=b_b_s[j],
                   wout=b_w_out[j].astype(BF16)) for j in range(b_w_in.shape[0])]
    return w


def _pack_state_vec(t):
    b, h, n = t.shape
    t = jnp.broadcast_to(t, (b, h, LANES)) if n == 1 else t
    return jnp.pad(t, ((0, 0), (0, 8 - h), (0, 0)))


def _trunk(x, w, c0, n0, m0, *, tm, nb, ff_chunks):
    bsz, s, d = x.shape
    m = bsz * s
    valid = min(s, LANES)
    mlp_len = min(MLP_CHUNK, s)
    reps = LANES // mlp_len
    xf = x.reshape(m, d)
    c_out, n_out, m_out, v_rows = [], [], [], []
    for i in range(w["depth"]):
        j = i // 2
        pre = None
        if i % 2 == 0:
            a = w["a"][j]
            if valid < LANES:
                xa = jnp.pad(xf.reshape(bsz, s, d), ((0, 0), (0, LANES - s), (0, 0))).reshape(bsz * LANES, d)
                sp = LANES
            else:
                xa, sp = xf, s
            q, kt, v, o, gc, gr = _a_inproj(xa, w["g_mix"][i], a["w"], a["wst"], a["br"], min(tm, sp), sp)
            hh, c_new, n_new, m_new = _mlstm(
                q.reshape(bsz, sp, QK_W), kt, v.reshape(bsz, sp, V_W), gc.reshape(bsz, sp, LANES), gr,
                c0[:, j], _pack_state_vec(n0[:, j]), _pack_state_vec(m0[:, j][..., None]), valid, nb)
            c_out.append(c_new)
            n_out.append(n_new[:, :N_HEADS, :])
            m_out.append(m_new[:, :N_HEADS, 0])
            pre = (hh[:, :s].reshape(m, V_W), o.reshape(bsz, sp, V_W)[:, :s].reshape(m, V_W), a["gout"],
                   a["wout"])
        else:
            bw = w["b"][j]
            mix = jnp.tile(bw["ws"][:, :mlp_len, :mlp_len], (1, reps, reps))
            bias = jnp.pad(jnp.tile(bw["bs"][:, :mlp_len], (1, reps)).T, ((0, 0), (0, LANES - MLP_GROUPS)))
            emit_v = s < MLP_CHUNK
            res = _bmix(xf, w["g_mix"][i], bw["win"], bw["gv"], mix, bias, bw["wout"],
                        min(tm, m), mlp_len, emit_v, 512)
            xf = res[0]
            if emit_v:
                v_rows.append(res[1].reshape(bsz, s, -1))
        g_final = w["g_final"] if i == w["depth"] - 1 else None
        xf = _ffn(xf, pre, w["g_ffn"][i], w["f_wg"][i], w["f_wu"][i], w["f_wd"][i], g_final,
                  min(tm, m), ff_chunks)
    return (xf.reshape(bsz, s, d), jnp.stack(c_out, axis=1), jnp.stack(n_out, axis=1),
            jnp.stack(m_out, axis=1), v_rows)


def kernel(x_prompt, x_sample, state_C, state_n, state_m, g_mix, g_ffn, g_final, a_w_in, a_b_gate, a_g_out,
           a_w_out, b_w_in, b_g_v, b_w_s, b_b_s, b_w_out, f_w_gu, f_w_down):
    w = _prep_weights(g_mix, g_ffn, g_final, a_w_in, a_b_gate, a_g_out, a_w_out,
                      b_w_in, b_g_v, b_w_s, b_b_s, b_w_out, f_w_gu, f_w_down)
    bp = x_prompt.shape[0]
    n_a = state_C.shape[1]
    zc = jnp.zeros((bp, n_a, N_HEADS, DV, DK), F32)
    zn = jnp.zeros((bp, n_a, N_HEADS, DK), F32)
    zm = jnp.zeros((bp, n_a, N_HEADS), F32)
    y_p, c_p, n_p, m_p, _ = _trunk(x_prompt, w, zc, zn, zm, tm=512, nb=4, ff_chunks=1)
    y_s, c_s, n_s, m_s, v_list = _trunk(x_sample, w, state_C.astype(F32), state_n.astype(F32),
                                        state_m.astype(F32), tm=512, nb=4, ff_chunks=1)
    v_rows = jnp.stack(v_list, axis=1)
    return (y_p, y_s, c_p, n_p, m_p, c_s, n_s, m_s, v_rows)
```

```python
import functools

import jax
import jax.numpy as jnp
from jax import lax
from jax.experimental import pallas as pl
from jax.experimental.pallas import tpu as pltpu

F32 = jnp.float32
BF16 = jnp.bfloat16

N_HEADS = 4
DK = 128
DV = 256
QK_W = N_HEADS * DK
V_W = N_HEADS * DV
GATE_CAP = 15.0
EPS = 1e-6
MLP_GROUPS = 4
MLP_CHUNK = 128
LANES = 128
GATE_ROWS = 16
VMEM_LIMIT = 56 * 1024 * 1024


def _resident(shape):
    nd = len(shape)
    return pl.BlockSpec(shape, lambda *_: (0,) * nd, pipeline_mode=pl.Buffered(1))


def _params(sem):
    return pltpu.CompilerParams(dimension_semantics=sem, vmem_limit_bytes=VMEM_LIMIT)


def _rms(x, g):
    return x * lax.rsqrt(jnp.mean(x * x, axis=-1, keepdims=True) + EPS) * g


def _dot(a, b):
    return jnp.dot(a, b, preferred_element_type=F32)


def _dot_nt(a, b):
    return lax.dot_general(a, b, (((1,), (1,)), ((), ())), preferred_element_type=F32)


def _split2(x):
    hi = x.astype(BF16)
    lo = (x - hi.astype(F32)).astype(BF16)
    return hi, lo


def _split3(x):
    p1 = x.astype(BF16)
    r = x - p1.astype(F32)
    p2 = r.astype(BF16)
    p3 = (r - p2.astype(F32)).astype(BF16)
    return p1, p2, p3


def _sigmoid(x):
    return 0.5 * (1.0 + jnp.tanh(0.5 * x))


def _log_sigmoid(x):
    return jnp.minimum(x, 0.0) - jnp.log1p(jnp.exp(-jnp.abs(x)))


def _gate_transform(pre, is_input_gate):
    g = GATE_CAP * jnp.tanh(pre / GATE_CAP)
    return jnp.where(is_input_gate, g, _log_sigmoid(g))


def _a_inproj_kernel(x_ref, g_ref, w_ref, wst_ref, br_ref, q_ref, kt_ref, v_ref, o_ref, gc_ref, gr_ref):
    L = LANES
    xn = _rms(x_ref[...], g_ref[...])
    xh, xl = _split2(xn)
    r_all = _dot_nt(wst_ref[...], xh)
    kt_ref[0] = r_all[:QK_W].astype(BF16)
    pre = (r_all[QK_W:QK_W + GATE_ROWS] + r_all[QK_W + GATE_ROWS:]
           + _dot_nt(wst_ref[QK_W:QK_W + GATE_ROWS, :], xl) + br_ref[:, :1])
    sub = lax.broadcasted_iota(jnp.int32, pre.shape, 0)
    gates = _gate_transform(pre, sub < 8)
    r_i = lax.broadcasted_iota(jnp.int32, (L, L), 0)
    c_i = lax.broadcasted_iota(jnp.int32, (L, L), 1)
    causal = c_i <= r_i
    eye = c_i == r_i
    triu = (r_i <= c_i).astype(BF16)
    for c in range(x_ref.shape[0] // L):
        sl = slice(c * L, (c + 1) * L)
        gch = gates[:, sl]
        r1, r2, r3 = _split3(gch)
        cum = _dot(r1, triu) + _dot(r2, triu) + _dot(r3, triu)
        b_r = cum[8:16]
        a_r = gch[0:8] - b_r
        gr_ref[0, :, sl] = a_r
        out = jnp.zeros((L, L), F32)
        for h in range(N_HEADS):
            a_b = jnp.broadcast_to(a_r[h:h + 1, :], (L, L))
            b_b = jnp.broadcast_to(b_r[h:h + 1, :], (L, L))
            cm_col = jnp.max(jnp.where(causal, a_b, -jnp.inf), axis=1, keepdims=True)
            a_col = jnp.sum(jnp.where(eye, a_b, 0.0), axis=1, keepdims=True)
            b_col = jnp.sum(jnp.where(eye, b_b, 0.0), axis=1, keepdims=True)
            out = jnp.where(c_i == N_HEADS + h, b_col, out)
            out = jnp.where(c_i == 2 * N_HEADS + h, cm_col, out)
            out = jnp.where(c_i == 3 * N_HEADS + h, a_col, out)
        gc_ref[sl, :] = out
    p = _dot(xh, w_ref[...])
    q_ref[...] = (p[:, :QK_W] * (DK ** -0.5)).astype(BF16)
    v_ref[...] = p[:, QK_W:QK_W + V_W].astype(BF16)
    o_ref[...] = p[:, QK_W + V_W:]


def _a_inproj(x, g, w, wst, br, tm, seq):
    m, d = x.shape
    n_p = w.shape[1]
    tpb = seq // tm
    row = lambda i: (i, 0)
    colb = lambda i: (i // tpb, 0, i % tpb)
    return pl.pallas_call(
        _a_inproj_kernel,
        grid=(m // tm,),
        in_specs=[pl.BlockSpec((tm, d), row), _resident((1, d)), _resident((d, n_p)),
                  _resident(wst.shape), _resident((GATE_ROWS, LANES))],
        out_specs=[pl.BlockSpec((tm, QK_W), row), pl.BlockSpec((1, QK_W, tm), colb),
                   pl.BlockSpec((tm, V_W), row), pl.BlockSpec((tm, V_W), row),
                   pl.BlockSpec((tm, LANES), row), pl.BlockSpec((1, 8, tm), colb)],
        out_shape=[jax.ShapeDtypeStruct((m, QK_W), BF16), jax.ShapeDtypeStruct((m // seq, QK_W, seq), BF16),
                   jax.ShapeDtypeStruct((m, V_W), BF16), jax.ShapeDtypeStruct((m, V_W), F32),
                   jax.ShapeDtypeStruct((m, LANES), F32), jax.ShapeDtypeStruct((m // seq, 8, seq), F32)],
        compiler_params=_params(("arbitrary",)),
        name="a_inproj",
    )(x, g, w, wst, br)


def _mlstm_kernel(q_ref, kt_ref, v_ref, gc_ref, gr_ref, c0_ref, n0_ref, m0_ref,
                  h_ref, c_ref, n_ref, m_ref, ct_s, n_s, m_s, *, valid):
    L = LANES
    nb = q_ref.shape[0]
    step = pl.program_id(1)

    @pl.when(step == 0)
    def _load_state():
        for bb in range(nb):
            for h in range(N_HEADS):
                si = bb * N_HEADS + h
                ct_s[si] = c0_ref[bb, h].T
                n_s[si] = jnp.broadcast_to(n0_ref[bb, h:h + 1, :], (L, DK)).T
                m_s[si] = jnp.broadcast_to(m0_ref[bb, h:h + 1, :], (8, LANES))

    row = lax.broadcasted_iota(jnp.int32, (L, L), 0)
    col = lax.broadcasted_iota(jnp.int32, (L, L), 1)
    causal = col <= row
    ones = jnp.ones((L, LANES), BF16)

    chains = [(bb, h) for bb in range(nb) for h in range(N_HEADS)]
    st = {}
    for bb, h in chains:
        si = bb * N_HEADS + h
        gc = gc_ref[bb]
        gr = gr_ref[bb]
        b_rep = jnp.broadcast_to(gc[:, N_HEADS + h:N_HEADS + h + 1], (L, LANES))
        cm_rep = jnp.broadcast_to(gc[:, 2 * N_HEADS + h:2 * N_HEADS + h + 1], (L, LANES))
        a_rep = jnp.broadcast_to(gc[:, 3 * N_HEADS + h:3 * N_HEADS + h + 1], (L, LANES))
        a_row = gr[h:h + 1, :]
        m_prev = m_s[si, 0:1, :]
        qh = q_ref[bb, :, h * DK:(h + 1) * DK]
        kth = kt_ref[bb, h * DK:(h + 1) * DK, :]
        mx = jnp.maximum(m_prev, cm_rep)
        s = _dot(qh, kth) * jnp.where(causal, jnp.exp(a_row - mx), 0.0)
        st[si] = dict(b_rep=b_rep, a_rep=a_rep, m_prev=m_prev, mx=mx, sb=s.astype(BF16),
                      w_inter=jnp.exp(m_prev - mx), qh=qh, kth=kth)
    for bb, h in chains:
        si = bb * N_HEADS + h
        c = st[si]
        ct = ct_s[si]
        n_rep = n_s[si]
        v0 = v_ref[bb, :, h * DV:h * DV + LANES]
        v1 = v_ref[bb, :, h * DV + LANES:(h + 1) * DV]
        ctb = ct.astype(BF16)
        den = _dot(c["sb"], ones) + c["w_inter"] * _dot(c["qh"], n_rep.astype(BF16))
        m_t = c["b_rep"] + c["mx"]
        r = 1.0 / jnp.maximum(jnp.abs(den), jnp.exp(-m_t))
        h_ref[bb, :, h * DV:h * DV + LANES] = (
            _dot(c["sb"], v0) + c["w_inter"] * _dot(c["qh"], ctb[:, :LANES])) * r
        h_ref[bb, :, h * DV + LANES:(h + 1) * DV] = (
            _dot(c["sb"], v1) + c["w_inter"] * _dot(c["qh"], ctb[:, LANES:])) * r
        c.update(m_t=m_t, ct=ct, n_rep=n_rep, v0=v0, v1=v1)
    for bb, h in chains:
        si = bb * N_HEADS + h
        c = st[si]
        m_new = c["m_t"][valid - 1:valid, :]
        b_last = c["b_rep"][valid - 1:valid, :]
        decay = jnp.exp(b_last + c["m_prev"] - m_new)
        w_s = jnp.exp(c["a_rep"] + (b_last - m_new))
        if valid < L:
            w_s = jnp.where(row < valid, w_s, 0.0)
        wv = jnp.concatenate([(w_s * c["v0"].astype(F32)).astype(BF16), (w_s * c["v1"].astype(F32)).astype(BF16)],
                             axis=1)
        ct_s[si] = jnp.concatenate([decay, decay], axis=1) * c["ct"] + _dot(c["kth"], wv)
        n_s[si] = decay * c["n_rep"] + _dot(c["kth"], w_s.astype(BF16))
        m_s[si] = jnp.broadcast_to(m_new, (8, LANES))

    @pl.when(step == pl.num_programs(1) - 1)
    def _store_state():
        for bb in range(nb):
            for h in range(N_HEADS):
                si = bb * N_HEADS + h
                c_ref[bb, h] = ct_s[si].T
                n_ref[bb, h:h + 1, :] = n_s[si].T[0:1, :]
                m_ref[bb, h:h + 1, :] = m_s[si, 0:1, :]
        n_ref[:, N_HEADS:, :] = jnp.zeros((nb, 8 - N_HEADS, LANES), F32)
        m_ref[:, N_HEADS:, :] = jnp.zeros((nb, 8 - N_HEADS, LANES), F32)


def _mlstm(q, kt, v, gc, gr, c0, n0, m0, valid, nb):
    b, s, _ = q.shape
    steps = s // LANES
    seq = lambda bi, i: (bi, i, 0)
    seq_t = lambda bi, i: (bi, 0, i)
    per_b3 = lambda bi, i: (bi, 0, 0)
    per_b4 = lambda bi, i: (bi, 0, 0, 0)
    return pl.pallas_call(
        functools.partial(_mlstm_kernel, valid=valid),
        grid=(b // nb, steps),
        in_specs=[pl.BlockSpec((nb, LANES, QK_W), seq), pl.BlockSpec((nb, QK_W, LANES), seq_t),
                  pl.BlockSpec((nb, LANES, V_W), seq),
                  pl.BlockSpec((nb, LANES, LANES), seq), pl.BlockSpec((nb, 8, LANES), seq_t),
                  pl.BlockSpec((nb, N_HEADS, DV, DK), per_b4), pl.BlockSpec((nb, 8, LANES), per_b3),
                  pl.BlockSpec((nb, 8, LANES), per_b3)],
        out_specs=[pl.BlockSpec((nb, LANES, V_W), seq), pl.BlockSpec((nb, N_HEADS, DV, DK), per_b4),
                   pl.BlockSpec((nb, 8, LANES), per_b3), pl.BlockSpec((nb, 8, LANES), per_b3)],
        out_shape=[jax.ShapeDtypeStruct((b, s, V_W), F32),
                   jax.ShapeDtypeStruct((b, N_HEADS, DV, DK), F32),
                   jax.ShapeDtypeStruct((b, 8, LANES), F32), jax.ShapeDtypeStruct((b, 8, LANES), F32)],
        scratch_shapes=[pltpu.VMEM((nb * N_HEADS, DK, DV), F32), pltpu.VMEM((nb * N_HEADS, DK, LANES), F32),
                        pltpu.VMEM((nb * N_HEADS, 8, LANES), F32)],
        compiler_params=_params(("arbitrary", "arbitrary")),
        name="mlstm_scan",
    )(q, kt, v, gc, gr, c0, n0, m0)


def _ffn_kernel(*refs, has_pre, final_norm, n_chunks):
    refs = list(refs)
    x_ref = refs.pop(0)
    if has_pre:
        h_ref, o_ref, go_ref, wo_ref = refs[:4]
        refs = refs[4:]
    g_ref, wg_ref, wu_ref, wd_ref = refs[:4]
    refs = refs[4:]
    if final_norm:
        gf_ref = refs.pop(0)
    out_ref, a_s = refs[:2]
    x = x_ref[...]
    if has_pre:
        y_s = refs[2]
        for hd in range(N_HEADS):
            cs = slice(hd * DV, (hd + 1) * DV)
            hh = h_ref[:, cs]
            hn = hh * lax.rsqrt(jnp.mean(hh * hh, axis=1, keepdims=True) + EPS)
            y_s[:, cs] = (hn * go_ref[:, cs] * _sigmoid(o_ref[:, cs])).astype(BF16)
        x = x + _dot(y_s[...], wo_ref[...])
    xb = _rms(x, g_ref[...]).astype(BF16)
    cw = wg_ref.shape[1] // n_chunks
    for c in range(n_chunks):
        cs = slice(c * cw, (c + 1) * cw)
        gate = _dot(xb, wg_ref[:, cs])
        up = _dot(xb, wu_ref[:, cs])
        a_s[:, cs] = (gate * _sigmoid(gate) * up).astype(BF16)
    out = x + _dot(a_s[...], wd_ref[...])
    if final_norm:
        out = _rms(out, gf_ref[...])
    out_ref[...] = out


def _ffn(x, pre, g, wg, wu, wd, g_final, tm, n_chunks):
    m, d = x.shape
    dff = wg.shape[1]
    row = lambda i: (i, 0)
    args, specs = [x], [pl.BlockSpec((tm, d), row)]
    scratch = [pltpu.VMEM((tm, dff), BF16)]
    if pre is not None:
        hh, o, gout, wo = pre
        args += [hh, o, gout, wo]
        specs += [pl.BlockSpec((tm, V_W), row), pl.BlockSpec((tm, V_W), row), _resident((1, V_W)),
                  _resident(wo.shape)]
        scratch.append(pltpu.VMEM((tm, V_W), BF16))
    args += [g, wg, wu, wd]
    specs += [_resident((1, d)), _resident((d, dff)), _resident((d, dff)), _resident((dff, d))]
    if g_final is not None:
        args.append(g_final)
        specs.append(_resident((1, d)))
    return pl.pallas_call(
        functools.partial(_ffn_kernel, has_pre=pre is not None, final_norm=g_final is not None,
                          n_chunks=n_chunks),
        grid=(m // tm,),
        in_specs=specs,
        out_specs=pl.BlockSpec((tm, d), row),
        out_shape=jax.ShapeDtypeStruct((m, d), F32),
        scratch_shapes=scratch,
        compiler_params=_params(("arbitrary",)),
        name="ffn",
    )(*args)


def _bmix_kernel(x_ref, g_ref, win_ref, gv_ref, mix_ref, bias_ref, wout_ref, *out_refs, chunk_len, emit_v,
                 sub):
    if emit_v:
        out_ref, v_ref, z_s = out_refs
    else:
        out_ref, z_s = out_refs
    L = LANES
    tm = x_ref.shape[0]
    inner = gv_ref.shape[1]
    dg = inner // MLP_GROUPS

    def gelu(t):
        return 0.5 * t * (1.0 + jnp.tanh(0.7978845608028654 * (t + 0.044715 * (t * t * t))))

    row = lax.broadcasted_iota(jnp.int32, (L, L), 0)
    col = lax.broadcasted_iota(jnp.int32, (L, L), 1)
    if chunk_len == L:
        mask = col <= row
    else:
        mask = (row // chunk_len == col // chunk_len) & (col % chunk_len <= row % chunk_len)
    mgs = [jnp.where(mask, mix_ref[gi], 0.0).astype(BF16) for gi in range(MLP_GROUPS)]

    for t0 in range(0, tm, sub):
        ts = slice(t0, t0 + sub)
        x = x_ref[ts, :]
        xb = _rms(x, g_ref[...]).astype(BF16)
        v = gelu(_dot(xb, win_ref[:, inner:]))
        u = gelu(_dot(xb, win_ref[:, :inner]))
        vc = v - jnp.mean(v, axis=-1, keepdims=True)
        v = vc * lax.rsqrt(jnp.mean(vc * vc, axis=-1, keepdims=True) + EPS) * gv_ref[...]
        if emit_v:
            v_ref[ts, :] = v
        vb = v.astype(BF16)
        for gi in range(MLP_GROUPS):
            bias = bias_ref[:, gi:gi + 1]
            gs = slice(gi * dg, (gi + 1) * dg)
            for c in range(sub // L):
                rs = slice(c * L, (c + 1) * L)
                sv = _dot(mgs[gi], vb[rs, gs]) + bias
                z_s[t0 + c * L:t0 + (c + 1) * L, gs] = (u[rs, gs] * sv).astype(BF16)
        out_ref[ts, :] = x + _dot(z_s[ts, :], wout_ref[...])


def _bmix(x, g, win, gv, mix, bias, wout, tm, chunk_len, emit_v, sub):
    m, d = x.shape
    inner = gv.shape[1]
    row = lambda i: (i, 0)
    out_specs = [pl.BlockSpec((tm, d), row)]
    out_shape = [jax.ShapeDtypeStruct((m, d), F32)]
    if emit_v:
        out_specs.append(pl.BlockSpec((tm, inner), row))
        out_shape.append(jax.ShapeDtypeStruct((m, inner), F32))
    return pl.pallas_call(
        functools.partial(_bmix_kernel, chunk_len=chunk_len, emit_v=emit_v, sub=min(sub, tm)),
        grid=(m // tm,),
        in_specs=[pl.BlockSpec((tm, d), row), _resident((1, d)), _resident(win.shape),
                  _resident((1, inner)), _resident(mix.shape), _resident(bias.shape),
                  _resident(wout.shape)],
        out_specs=out_specs,
        out_shape=out_shape,
        scratch_shapes=[pltpu.VMEM((tm, inner), BF16)],
        compiler_params=_params(("arbitrary",)),
        name="chunk_mlp",
    )(x, g, win, gv, mix, bias, wout)


def _gate_rows(t):
    z = jnp.zeros((8 - N_HEADS, t.shape[1]), t.dtype)
    return jnp.concatenate([t[:N_HEADS], z, t[N_HEADS:], z], axis=0)


def _prep_weights(g_mix, g_ffn, g_final, a_w_in, a_b_gate, a_g_out, a_w_out,
                  b_w_in, b_g_v, b_w_s, b_b_s, b_w_out, f_w_gu, f_w_down):
    depth = g_mix.shape[0]
    n_proj = 2 * QK_W + 2 * V_W
    d_ff = f_w_down.shape[1]
    w = dict(depth=depth, g_final=g_final.reshape(1, -1))
    w["g_mix"] = [g_mix[i].reshape(1, -1) for i in range(depth)]
    w["g_ffn"] = [g_ffn[i].reshape(1, -1) for i in range(depth)]
    w["f_wg"] = [f_w_gu[i, :, :d_ff].astype(BF16) for i in range(depth)]
    w["f_wu"] = [f_w_gu[i, :, d_ff:].astype(BF16) for i in range(depth)]
    w["f_wd"] = [f_w_down[i].astype(BF16) for i in range(depth)]
    a = []
    for j in range(a_w_in.shape[0]):
        wk_t = a_w_in[j, :, QK_W:2 * QK_W].T.astype(BF16)
        wg_hi, wg_lo = _split2(_gate_rows(a_w_in[j, :, n_proj:].T))
        a.append(dict(
            w=jnp.concatenate([a_w_in[j, :, :QK_W], a_w_in[j, :, 2 * QK_W:n_proj]], axis=1).astype(BF16),
            wst=jnp.concatenate([wk_t, wg_hi, wg_lo], axis=0),
            br=jnp.broadcast_to(_gate_rows(a_b_gate[j][:, None]), (GATE_ROWS, LANES)),
            gout=a_g_out[j].reshape(1, -1),
            wout=a_w_out[j].astype(BF16)))
    w["a"] = a
    w["b"] = [dict(win=b_w_in[j].astype(BF16), gv=b_g_v[j].reshape(1, -1), ws=b_w_s[j], bs=b_b_s[j],
                   wout=b_w_out[j].astype(BF16)) for j in range(b_w_in.shape[0])]
    return w


def _pack_state_vec(t):
    b, h, n = t.shape
    t = jnp.broadcast_to(t, (b, h, LANES)) if n == 1 else t
    return jnp.pad(t, ((0, 0), (0, 8 - h), (0, 0)))


def _trunk(x, w, c0, n0, m0, *, tm, nb, ff_chunks):
    bsz, s, d = x.shape
    m = bsz * s
    valid = min(s, LANES)
    mlp_len = min(MLP_CHUNK, s)
    reps = LANES // mlp_len
    xf = x.reshape(m, d)
    c_out, n_out, m_out, v_rows = [], [], [], []
    for i in range(w["depth"]):
        j = i // 2
        pre = None
        if i % 2 == 0:
            a = w["a"][j]
            if valid < LANES:
                xa = jnp.pad(xf.reshape(bsz, s, d), ((0, 0), (0, LANES - s), (0, 0))).reshape(bsz * LANES, d)
                sp = LANES
            else:
                xa, sp = xf, s
            q, kt, v, o, gc, gr = _a_inproj(xa, w["g_mix"][i], a["w"], a["wst"], a["br"], min(tm, sp), sp)
            hh, c_new, n_new, m_new = _mlstm(
                q.reshape(bsz, sp, QK_W), kt, v.reshape(bsz, sp, V_W), gc.reshape(bsz, sp, LANES), gr,
                c0[:, j], _pack_state_vec(n0[:, j]), _pack_state_vec(m0[:, j][..., None]), valid, nb)
            c_out.append(c_new)
            n_out.append(n_new[:, :N_HEADS, :])
            m_out.append(m_new[:, :N_HEADS, 0])
            pre = (hh[:, :s].reshape(m, V_W), o.reshape(bsz, sp, V_W)[:, :s].reshape(m, V_W), a["gout"],
                   a["wout"])
        else:
            bw = w["b"][j]
            mix = jnp.tile(bw["ws"][:, :mlp_len, :mlp_len], (1, reps, reps))
            bias = jnp.pad(jnp.tile(bw["bs"][:, :mlp_len], (1, reps)).T, ((0, 0), (0, LANES - MLP_GROUPS)))
            emit_v = s < MLP_CHUNK
            res = _bmix(xf, w["g_mix"][i], bw["win"], bw["gv"], mix, bias, bw["wout"],
                        min(tm, m), mlp_len, emit_v, 512)
            xf = res[0]
            if emit_v:
                v_rows.append(res[1].reshape(bsz, s, -1))
        g_final = w["g_final"] if i == w["depth"] - 1 else None
        xf = _ffn(xf, pre, w["g_ffn"][i], w["f_wg"][i], w["f_wu"][i], w["f_wd"][i], g_final,
                  min(tm, m), ff_chunks)
    return (xf.reshape(bsz, s, d), jnp.stack(c_out, axis=1), jnp.stack(n_out, axis=1),
            jnp.stack(m_out, axis=1), v_rows)


def kernel(x_prompt, x_sample, state_C, state_n, state_m, g_mix, g_ffn, g_final, a_w_in, a_b_gate, a_g_out,
           a_w_out, b_w_in, b_g_v, b_w_s, b_b_s, b_w_out, f_w_gu, f_w_down):
    w = _prep_weights(g_mix, g_ffn, g_final, a_w_in, a_b_gate, a_g_out, a_w_out,
                      b_w_in, b_g_v, b_w_s, b_b_s, b_w_out, f_w_gu, f_w_down)
    bp = x_prompt.shape[0]
    n_a = state_C.shape[1]
    zc = jnp.zeros((bp, n_a, N_HEADS, DV, DK), F32)
    zn = jnp.zeros((bp, n_a, N_HEADS, DK), F32)
    zm = jnp.zeros((bp, n_a, N_HEADS), F32)
    y_p, c_p, n_p, m_p, _ = _trunk(x_prompt, w, zc, zn, zm, tm=512, nb=4, ff_chunks=1)
    y_s, c_s, n_s, m_s, v_list = _trunk(x_sample, w, state_C.astype(F32), state_n.astype(F32),
                                        state_m.astype(F32), tm=512, nb=4, ff_chunks=1)
    v_rows = jnp.stack(v_list, axis=1)
    return (y_p, y_s, c_p, n_p, m_p, c_s, n_s, m_s, v_rows)
```

```python
import functools

import jax
import jax.numpy as jnp
from jax import lax
from jax.experimental import pallas as pl
from jax.experimental.pallas import tpu as pltpu

F32 = jnp.float32
BF16 = jnp.bfloat16

N_HEADS = 4
DK = 128
DV = 256
QK_W = N_HEADS * DK
V_W = N_HEADS * DV
GATE_CAP = 15.0
EPS = 1e-6
MLP_GROUPS = 4
MLP_CHUNK = 128
LANES = 128
GATE_ROWS = 16
VMEM_LIMIT = 56 * 1024 * 1024


def _resident(shape):
    nd = len(shape)
    return pl.BlockSpec(shape, lambda *_: (0,) * nd, pipeline_mode=pl.Buffered(1))


def _layer_resident(stacked, layer):
    return pl.BlockSpec((None,) + tuple(stacked.shape[1:]), lambda *_: (layer, 0, 0),
                        pipeline_mode=pl.Buffered(1))


def _params(sem):
    return pltpu.CompilerParams(dimension_semantics=sem, vmem_limit_bytes=VMEM_LIMIT)


def _rms(x, g):
    return x * lax.rsqrt(jnp.mean(x * x, axis=-1, keepdims=True) + EPS) * g


def _dot(a, b):
    return jnp.dot(a, b, preferred_element_type=F32)


def _dot_nt(a, b):
    return lax.dot_general(a, b, (((1,), (1,)), ((), ())), preferred_element_type=F32)


def _split2(x):
    hi = x.astype(BF16)
    lo = (x - hi.astype(F32)).astype(BF16)
    return hi, lo


def _split3(x):
    p1 = x.astype(BF16)
    r = x - p1.astype(F32)
    p2 = r.astype(BF16)
    p3 = (r - p2.astype(F32)).astype(BF16)
    return p1, p2, p3


def _sigmoid(x):
    return 0.5 * (1.0 + jnp.tanh(0.5 * x))


def _log_sigmoid(x):
    return jnp.minimum(x, 0.0) - jnp.log1p(jnp.exp(-jnp.abs(x)))


def _gate_transform(pre, is_input_gate):
    g = GATE_CAP * jnp.tanh(pre / GATE_CAP)
    return jnp.where(is_input_gate, g, _log_sigmoid(g))


def _a_inproj_kernel(x_ref, g_ref, w_ref, wst_ref, br_ref, q_ref, kt_ref, v_ref, o_ref, gc_ref, gr_ref):
    L = LANES
    xn = _rms(x_ref[...], g_ref[...])
    xh, xl = _split2(xn)
    r_all = _dot_nt(wst_ref[...], xh)
    kt_ref[0] = r_all[:QK_W].astype(BF16)
    pre = (r_all[QK_W:QK_W + GATE_ROWS] + r_all[QK_W + GATE_ROWS:]
           + _dot_nt(wst_ref[QK_W:QK_W + GATE_ROWS, :], xl) + br_ref[:, :1])
    sub = lax.broadcasted_iota(jnp.int32, pre.shape, 0)
    gates = _gate_transform(pre, sub < 8)
    r_i = lax.broadcasted_iota(jnp.int32, (L, L), 0)
    c_i = lax.broadcasted_iota(jnp.int32, (L, L), 1)
    causal = c_i <= r_i
    eye = c_i == r_i
    triu = (r_i <= c_i).astype(BF16)
    for c in range(x_ref.shape[0] // L):
        sl = slice(c * L, (c + 1) * L)
        gch = gates[:, sl]
        r1, r2, r3 = _split3(gch)
        cum = _dot(r1, triu) + _dot(r2, triu) + _dot(r3, triu)
        b_r = cum[8:16]
        a_r = gch[0:8] - b_r
        gr_ref[0, :, sl] = a_r
        out = jnp.zeros((L, L), F32)
        for h in range(N_HEADS):
            a_b = jnp.broadcast_to(a_r[h:h + 1, :], (L, L))
            b_b = jnp.broadcast_to(b_r[h:h + 1, :], (L, L))
            cm_col = jnp.max(jnp.where(causal, a_b, -jnp.inf), axis=1, keepdims=True)
            a_col = jnp.sum(jnp.where(eye, a_b, 0.0), axis=1, keepdims=True)
            b_col = jnp.sum(jnp.where(eye, b_b, 0.0), axis=1, keepdims=True)
            out = jnp.where(c_i == N_HEADS + h, b_col, out)
            out = jnp.where(c_i == 2 * N_HEADS + h, cm_col, out)
            out = jnp.where(c_i == 3 * N_HEADS + h, a_col, out)
        gc_ref[sl, :] = out
    p = _dot(xh, w_ref[...])
    q_ref[...] = (p[:, :QK_W] * (DK ** -0.5)).astype(BF16)
    v_ref[...] = p[:, QK_W:QK_W + V_W].astype(BF16)
    o_ref[...] = p[:, QK_W + V_W:]


def _a_inproj(x, g, w, wst, br, tm, seq):
    m, d = x.shape
    n_p = w.shape[1]
    tpb = seq // tm
    row = lambda i: (i, 0)
    colb = lambda i: (i // tpb, 0, i % tpb)
    return pl.pallas_call(
        _a_inproj_kernel,
        grid=(m // tm,),
        in_specs=[pl.BlockSpec((tm, d), row), _resident((1, d)), _resident((d, n_p)),
                  _resident(wst.shape), _resident((GATE_ROWS, LANES))],
        out_specs=[pl.BlockSpec((tm, QK_W), row), pl.BlockSpec((1, QK_W, tm), colb),
                   pl.BlockSpec((tm, V_W), row), pl.BlockSpec((tm, V_W), row),
                   pl.BlockSpec((tm, LANES), row), pl.BlockSpec((1, 8, tm), colb)],
        out_shape=[jax.ShapeDtypeStruct((m, QK_W), BF16), jax.ShapeDtypeStruct((m // seq, QK_W, seq), BF16),
                   jax.ShapeDtypeStruct((m, V_W), BF16), jax.ShapeDtypeStruct((m, V_W), F32),
                   jax.ShapeDtypeStruct((m, LANES), F32), jax.ShapeDtypeStruct((m // seq, 8, seq), F32)],
        compiler_params=_params(("arbitrary",)),
        name="a_inproj",
    )(x, g, w, wst, br)


def _mlstm_kernel(q_ref, kt_ref, v_ref, gc_ref, gr_ref, c0_ref, n0_ref, m0_ref,
                  h_ref, c_ref, n_ref, m_ref, ct_s, n_s, m_s, *, valid):
    L = LANES
    nb = q_ref.shape[0]
    step = pl.program_id(1)

    @pl.when(step == 0)
    def _load_state():
        for bb in range(nb):
            for h in range(N_HEADS):
                si = bb * N_HEADS + h
                ct_s[si] = c0_ref[bb, h].T
                n_s[si] = jnp.broadcast_to(n0_ref[bb, h:h + 1, :], (L, DK)).T
                m_s[si] = jnp.broadcast_to(m0_ref[bb, h:h + 1, :], (8, LANES))

    row = lax.broadcasted_iota(jnp.int32, (L, L), 0)
    col = lax.broadcasted_iota(jnp.int32, (L, L), 1)
    causal = col <= row
    ones = jnp.ones((L, LANES), BF16)

    chains = [(bb, h) for bb in range(nb) for h in range(N_HEADS)]
    st = {}
    for bb, h in chains:
        si = bb * N_HEADS + h
        gc = gc_ref[bb]
        gr = gr_ref[bb]
        b_rep = jnp.broadcast_to(gc[:, N_HEADS + h:N_HEADS + h + 1], (L, LANES))
        cm_rep = jnp.broadcast_to(gc[:, 2 * N_HEADS + h:2 * N_HEADS + h + 1], (L, LANES))
        a_rep = jnp.broadcast_to(gc[:, 3 * N_HEADS + h:3 * N_HEADS + h + 1], (L, LANES))
        a_row = gr[h:h + 1, :]
        m_prev = m_s[si, 0:1, :]
        qh = q_ref[bb, :, h * DK:(h + 1) * DK]
        kth = kt_ref[bb, h * DK:(h + 1) * DK, :]
        mx = jnp.maximum(m_prev, cm_rep)
        s = _dot(qh, kth) * jnp.where(causal, jnp.exp(a_row - mx), 0.0)
        st[si] = dict(b_rep=b_rep, a_rep=a_rep, m_prev=m_prev, mx=mx, sb=s.astype(BF16),
                      w_inter=jnp.exp(m_prev - mx), qh=qh, kth=kth)
    for bb, h in chains:
        si = bb * N_HEADS + h
        c = st[si]
        ct = ct_s[si]
        n_rep = n_s[si]
        v0 = v_ref[bb, :, h * DV:h * DV + LANES]
        v1 = v_ref[bb, :, h * DV + LANES:(h + 1) * DV]
        ctb = ct.astype(BF16)
        den = _dot(c["sb"], ones) + c["w_inter"] * _dot(c["qh"], n_rep.astype(BF16))
        m_t = c["b_rep"] + c["mx"]
        r = 1.0 / jnp.maximum(jnp.abs(den), jnp.exp(-m_t))
        h_ref[bb, :, h * DV:h * DV + LANES] = (
            _dot(c["sb"], v0) + c["w_inter"] * _dot(c["qh"], ctb[:, :LANES])) * r
        h_ref[bb, :, h * DV + LANES:(h + 1) * DV] = (
            _dot(c["sb"], v1) + c["w_inter"] * _dot(c["qh"], ctb[:, LANES:])) * r
        c.update(m_t=m_t, ct=ct, n_rep=n_rep, v0=v0, v1=v1)
    for bb, h in chains:
        si = bb * N_HEADS + h
        c = st[si]
        m_new = c["m_t"][valid - 1:valid, :]
        b_last = c["b_rep"][valid - 1:valid, :]
        decay = jnp.exp(b_last + c["m_prev"] - m_new)
        w_s = jnp.exp(c["a_rep"] + (b_last - m_new))
        if valid < L:
            w_s = jnp.where(row < valid, w_s, 0.0)
        wv = jnp.concatenate([(w_s * c["v0"].astype(F32)).astype(BF16), (w_s * c["v1"].astype(F32)).astype(BF16)],
                             axis=1)
        ct_s[si] = jnp.concatenate([decay, decay], axis=1) * c["ct"] + _dot(c["kth"], wv)
        n_s[si] = decay * c["n_rep"] + _dot(c["kth"], w_s.astype(BF16))
        m_s[si] = jnp.broadcast_to(m_new, (8, LANES))

    @pl.when(step == pl.num_programs(1) - 1)
    def _store_state():
        for bb in range(nb):
            for h in range(N_HEADS):
                si = bb * N_HEADS + h
                c_ref[bb, h] = ct_s[si].T
                n_ref[bb, h:h + 1, :] = n_s[si].T[0:1, :]
                m_ref[bb, h:h + 1, :] = m_s[si, 0:1, :]
        n_ref[:, N_HEADS:, :] = jnp.zeros((nb, 8 - N_HEADS, LANES), F32)
        m_ref[:, N_HEADS:, :] = jnp.zeros((nb, 8 - N_HEADS, LANES), F32)


def _mlstm(q, kt, v, gc, gr, c0, n0, m0, valid, nb):
    b, s, _ = q.shape
    steps = s // LANES
    seq = lambda bi, i: (bi, i, 0)
    seq_t = lambda bi, i: (bi, 0, i)
    per_b3 = lambda bi, i: (bi, 0, 0)
    per_b4 = lambda bi, i: (bi, 0, 0, 0)
    return pl.pallas_call(
        functools.partial(_mlstm_kernel, valid=valid),
        grid=(b // nb, steps),
        in_specs=[pl.BlockSpec((nb, LANES, QK_W), seq), pl.BlockSpec((nb, QK_W, LANES), seq_t),
                  pl.BlockSpec((nb, LANES, V_W), seq),
                  pl.BlockSpec((nb, LANES, LANES), seq), pl.BlockSpec((nb, 8, LANES), seq_t),
                  pl.BlockSpec((nb, N_HEADS, DV, DK), per_b4), pl.BlockSpec((nb, 8, LANES), per_b3),
                  pl.BlockSpec((nb, 8, LANES), per_b3)],
        out_specs=[pl.BlockSpec((nb, LANES, V_W), seq), pl.BlockSpec((nb, N_HEADS, DV, DK), per_b4),
                   pl.BlockSpec((nb, 8, LANES), per_b3), pl.BlockSpec((nb, 8, LANES), per_b3)],
        out_shape=[jax.ShapeDtypeStruct((b, s, V_W), F32),
                   jax.ShapeDtypeStruct((b, N_HEADS, DV, DK), F32),
                   jax.ShapeDtypeStruct((b, 8, LANES), F32), jax.ShapeDtypeStruct((b, 8, LANES), F32)],
        scratch_shapes=[pltpu.VMEM((nb * N_HEADS, DK, DV), F32), pltpu.VMEM((nb * N_HEADS, DK, LANES), F32),
                        pltpu.VMEM((nb * N_HEADS, 8, LANES), F32)],
        compiler_params=_params(("arbitrary", "arbitrary")),
        name="mlstm_scan",
    )(q, kt, v, gc, gr, c0, n0, m0)


def _ffn_kernel(*refs, has_pre, final_norm, n_chunks):
    refs = list(refs)
    x_ref = refs.pop(0)
    if has_pre:
        h_ref, o_ref, go_ref, wo_ref = refs[:4]
        refs = refs[4:]
    g_ref, wgu_ref, wd_ref = refs[:3]
    refs = refs[3:]
    if final_norm:
        gf_ref = refs.pop(0)
    out_ref, a_s = refs[:2]
    x = x_ref[...]
    if has_pre:
        y_s = refs[2]
        for hd in range(N_HEADS):
            cs = slice(hd * DV, (hd + 1) * DV)
            hh = h_ref[:, cs]
            hn = hh * lax.rsqrt(jnp.mean(hh * hh, axis=1, keepdims=True) + EPS)
            y_s[:, cs] = (hn * go_ref[:, cs] * _sigmoid(o_ref[:, cs])).astype(BF16)
        x = x + _dot(y_s[...], wo_ref[...])
    xb = _rms(x, g_ref[...]).astype(BF16)
    dff = wd_ref.shape[0]
    cw = dff // n_chunks
    for c in range(n_chunks):
        cs = slice(c * cw, (c + 1) * cw)
        gate = _dot(xb, wgu_ref[:, cs])
        up = _dot(xb, wgu_ref[:, dff + c * cw:dff + (c + 1) * cw])
        a_s[:, cs] = (gate * _sigmoid(gate) * up).astype(BF16)
    out = x + _dot(a_s[...], wd_ref[...])
    if final_norm:
        out = _rms(out, gf_ref[...])
    out_ref[...] = out


def _ffn(x, pre, g, wgu, wd, layer, g_final, tm, n_chunks):
    m, d = x.shape
    dff = wd.shape[1]
    row = lambda i: (i, 0)
    args, specs = [x], [pl.BlockSpec((tm, d), row)]
    scratch = [pltpu.VMEM((tm, dff), BF16)]
    if pre is not None:
        hh, o, gout, wo = pre
        args += [hh, o, gout, wo]
        specs += [pl.BlockSpec((tm, V_W), row), pl.BlockSpec((tm, V_W), row), _resident((1, V_W)),
                  _resident(wo.shape)]
        scratch.append(pltpu.VMEM((tm, V_W), BF16))
    args += [g, wgu, wd]
    specs += [_resident((1, d)), _layer_resident(wgu, layer), _layer_resident(wd, layer)]
    if g_final is not None:
        args.append(g_final)
        specs.append(_resident((1, d)))
    return pl.pallas_call(
        functools.partial(_ffn_kernel, has_pre=pre is not None, final_norm=g_final is not None,
                          n_chunks=n_chunks),
        grid=(m // tm,),
        in_specs=specs,
        out_specs=pl.BlockSpec((tm, d), row),
        out_shape=jax.ShapeDtypeStruct((m, d), F32),
        scratch_shapes=scratch,
        compiler_params=_params(("arbitrary",)),
        name="ffn",
    )(*args)


def _bmix_kernel(x_ref, g_ref, win_ref, gv_ref, mix_ref, bias_ref, wout_ref, *out_refs, chunk_len, emit_v,
                 sub):
    if emit_v:
        out_ref, v_ref, z_s = out_refs
    else:
        out_ref, z_s = out_refs
    L = LANES
    tm = x_ref.shape[0]
    inner = gv_ref.shape[1]
    dg = inner // MLP_GROUPS

    def gelu(t):
        return 0.5 * t * (1.0 + jnp.tanh(0.7978845608028654 * (t + 0.044715 * (t * t * t))))

    row = lax.broadcasted_iota(jnp.int32, (L, L), 0)
    col = lax.broadcasted_iota(jnp.int32, (L, L), 1)
    if chunk_len == L:
        mask = col <= row
    else:
        mask = (row // chunk_len == col // chunk_len) & (col % chunk_len <= row % chunk_len)
    mgs = [jnp.where(mask, mix_ref[gi], 0.0).astype(BF16) for gi in range(MLP_GROUPS)]

    for t0 in range(0, tm, sub):
        ts = slice(t0, t0 + sub)
        x = x_ref[ts, :]
        xb = _rms(x, g_ref[...]).astype(BF16)
        v = gelu(_dot(xb, win_ref[:, inner:]))
        u = gelu(_dot(xb, win_ref[:, :inner]))
        vc = v - jnp.mean(v, axis=-1, keepdims=True)
        v = vc * lax.rsqrt(jnp.mean(vc * vc, axis=-1, keepdims=True) + EPS) * gv_ref[...]
        if emit_v:
            v_ref[ts, :] = v
        vb = v.astype(BF16)
        for gi in range(MLP_GROUPS):
            bias = bias_ref[:, gi:gi + 1]
            gs = slice(gi * dg, (gi + 1) * dg)
            for c in range(sub // L):
                rs = slice(c * L, (c + 1) * L)
                sv = _dot(mgs[gi], vb[rs, gs]) + bias
                z_s[t0 + c * L:t0 + (c + 1) * L, gs] = (u[rs, gs] * sv).astype(BF16)
        out_ref[ts, :] = x + _dot(z_s[ts, :], wout_ref[...])


def _bmix(x, g, win, gv, mix, bias, wout, layer, tm, chunk_len, emit_v, sub):
    m, d = x.shape
    inner = gv.shape[1]
    row = lambda i: (i, 0)
    out_specs = [pl.BlockSpec((tm, d), row)]
    out_shape = [jax.ShapeDtypeStruct((m, d), F32)]
    if emit_v:
        out_specs.append(pl.BlockSpec((tm, inner), row))
        out_shape.append(jax.ShapeDtypeStruct((m, inner), F32))
    return pl.pallas_call(
        functools.partial(_bmix_kernel, chunk_len=chunk_len, emit_v=emit_v, sub=min(sub, tm)),
        grid=(m // tm,),
        in_specs=[pl.BlockSpec((tm, d), row), _resident((1, d)), _layer_resident(win, layer),
                  _resident((1, inner)), _resident(mix.shape), _resident(bias.shape),
                  _layer_resident(wout, layer)],
        out_specs=out_specs,
        out_shape=out_shape,
        scratch_shapes=[pltpu.VMEM((tm, inner), BF16)],
        compiler_params=_params(("arbitrary",)),
        name="chunk_mlp",
    )(x, g, win, gv, mix, bias, wout)


def _gate_rows(t):
    z = jnp.zeros((8 - N_HEADS, t.shape[1]), t.dtype)
    return jnp.concatenate([t[:N_HEADS], z, t[N_HEADS:], z], axis=0)


def _prep_weights(g_mix, g_ffn, g_final, a_w_in, a_b_gate, a_g_out, a_w_out,
                  b_w_in, b_g_v, b_w_s, b_b_s, b_w_out, f_w_gu, f_w_down):
    depth = g_mix.shape[0]
    n_proj = 2 * QK_W + 2 * V_W
    w = dict(depth=depth, g_final=g_final.reshape(1, -1))
    w["g_mix"] = [g_mix[i].reshape(1, -1) for i in range(depth)]
    w["g_ffn"] = [g_ffn[i].reshape(1, -1) for i in range(depth)]
    w["f_wgu"] = f_w_gu.astype(BF16)
    w["f_wd"] = f_w_down.astype(BF16)
    w["b_win"] = b_w_in.astype(BF16)
    w["b_wout"] = b_w_out.astype(BF16)
    a = []
    for j in range(a_w_in.shape[0]):
        wk_t = a_w_in[j, :, QK_W:2 * QK_W].T.astype(BF16)
        wg_hi, wg_lo = _split2(_gate_rows(a_w_in[j, :, n_proj:].T))
        a.append(dict(
            w=jnp.concatenate([a_w_in[j, :, :QK_W], a_w_in[j, :, 2 * QK_W:n_proj]], axis=1).astype(BF16),
            wst=jnp.concatenate([wk_t, wg_hi, wg_lo], axis=0),
            br=jnp.broadcast_to(_gate_rows(a_b_gate[j][:, None]), (GATE_ROWS, LANES)),
            gout=a_g_out[j].reshape(1, -1),
            wout=a_w_out[j].astype(BF16)))
    w["a"] = a
    w["b"] = [dict(gv=b_g_v[j].reshape(1, -1), ws=b_w_s[j], bs=b_b_s[j]) for j in range(b_w_in.shape[0])]
    return w


def _pack_state_vec(t):
    b, h, n = t.shape
    t = jnp.broadcast_to(t, (b, h, LANES)) if n == 1 else t
    return jnp.pad(t, ((0, 0), (0, 8 - h), (0, 0)))


def _trunk(x, w, c0, n0, m0, *, tm, nb, ff_chunks_wide):
    bsz, s, d = x.shape
    m = bsz * s
    valid = min(s, LANES)
    mlp_len = min(MLP_CHUNK, s)
    reps = LANES // mlp_len
    xf = x.reshape(m, d)
    c_out, n_out, m_out, v_rows = [], [], [], []
    for i in range(w["depth"]):
        j = i // 2
        pre = None
        if i % 2 == 0:
            a = w["a"][j]
            if valid < LANES:
                xa = jnp.pad(xf.reshape(bsz, s, d), ((0, 0), (0, LANES - s), (0, 0))).reshape(bsz * LANES, d)
                sp = LANES
            else:
                xa, sp = xf, s
            q, kt, v, o, gc, gr = _a_inproj(xa, w["g_mix"][i], a["w"], a["wst"], a["br"], min(2 * tm, sp), sp)
            hh, c_new, n_new, m_new = _mlstm(
                q.reshape(bsz, sp, QK_W), kt, v.reshape(bsz, sp, V_W), gc.reshape(bsz, sp, LANES), gr,
                c0[:, j], _pack_state_vec(n0[:, j]), _pack_state_vec(m0[:, j][..., None]), valid, nb)
            c_out.append(c_new)
            n_out.append(n_new[:, :N_HEADS, :])
            m_out.append(m_new[:, :N_HEADS, 0])
            pre = (hh[:, :s].reshape(m, V_W), o.reshape(bsz, sp, V_W)[:, :s].reshape(m, V_W), a["gout"],
                   a["wout"])
        else:
            bw = w["b"][j]
            mix = jnp.tile(bw["ws"][:, :mlp_len, :mlp_len], (1, reps, reps))
            bias = jnp.pad(jnp.tile(bw["bs"][:, :mlp_len], (1, reps)).T, ((0, 0), (0, LANES - MLP_GROUPS)))
            emit_v = s < MLP_CHUNK
            res = _bmix(xf, w["g_mix"][i], w["b_win"], bw["gv"], mix, bias, w["b_wout"], j,
                        min(tm, m), mlp_len, emit_v, 512)
            xf = res[0]
            if emit_v:
                v_rows.append(res[1].reshape(bsz, s, -1))
        g_final = w["g_final"] if i == w["depth"] - 1 else None
        wide = pre is None
        xf = _ffn(xf, pre, w["g_ffn"][i], w["f_wgu"], w["f_wd"], i, g_final,
                  min(2 * tm if wide else tm, m), ff_chunks_wide if wide else 1)
    return (xf.reshape(bsz, s, d), jnp.stack(c_out, axis=1), jnp.stack(n_out, axis=1),
            jnp.stack(m_out, axis=1), v_rows)


def kernel(x_prompt, x_sample, state_C, state_n, state_m, g_mix, g_ffn, g_final, a_w_in, a_b_gate, a_g_out,
           a_w_out, b_w_in, b_g_v, b_w_s, b_b_s, b_w_out, f_w_gu, f_w_down):
    w = _prep_weights(g_mix, g_ffn, g_final, a_w_in, a_b_gate, a_g_out, a_w_out,
                      b_w_in, b_g_v, b_w_s, b_b_s, b_w_out, f_w_gu, f_w_down)
    bp = x_prompt.shape[0]
    n_a = state_C.shape[1]
    zc = jnp.zeros((bp, n_a, N_HEADS, DV, DK), F32)
    zn = jnp.zeros((bp, n_a, N_HEADS, DK), F32)
    zm = jnp.zeros((bp, n_a, N_HEADS), F32)
    y_p, c_p, n_p, m_p, _ = _trunk(x_prompt, w, zc, zn, zm, tm=512, nb=4, ff_chunks_wide=11)
    y_s, c_s, n_s, m_s, v_list = _trunk(x_sample, w, state_C.astype(F32), state_n.astype(F32),
                                        state_m.astype(F32), tm=512, nb=4, ff_chunks_wide=11)
    v_rows = jnp.stack(v_list, axis=1)
    return (y_p, y_s, c_p, n_p, m_p, c_s, n_s, m_s, v_rows)
```

```python
import functools

import jax
import jax.numpy as jnp
from jax import lax
from jax.experimental import pallas as pl
from jax.experimental.pallas import tpu as pltpu

F32 = jnp.float32
BF16 = jnp.bfloat16

N_HEADS = 4
DK = 128
DV = 256
QK_W = N_HEADS * DK
V_W = N_HEADS * DV
GATE_CAP = 15.0
EPS = 1e-6
MLP_GROUPS = 4
MLP_CHUNK = 128
LANES = 128
GATE_ROWS = 16
VMEM_LIMIT = 56 * 1024 * 1024


def _resident(shape):
    nd = len(shape)
    return pl.BlockSpec(shape, lambda *_: (0,) * nd, pipeline_mode=pl.Buffered(1))


def _layer_resident(stacked, layer):
    return pl.BlockSpec((None,) + tuple(stacked.shape[1:]), lambda *_: (layer, 0, 0),
                        pipeline_mode=pl.Buffered(1))


def _params(sem):
    return pltpu.CompilerParams(dimension_semantics=sem, vmem_limit_bytes=VMEM_LIMIT)


def _rms(x, g):
    return x * lax.rsqrt(jnp.mean(x * x, axis=-1, keepdims=True) + EPS) * g


def _dot(a, b):
    return jnp.dot(a, b, preferred_element_type=F32)


def _dot_nt(a, b):
    return lax.dot_general(a, b, (((1,), (1,)), ((), ())), preferred_element_type=F32)


def _split2(x):
    hi = x.astype(BF16)
    lo = (x - hi.astype(F32)).astype(BF16)
    return hi, lo


def _split3(x):
    p1 = x.astype(BF16)
    r = x - p1.astype(F32)
    p2 = r.astype(BF16)
    p3 = (r - p2.astype(F32)).astype(BF16)
    return p1, p2, p3


def _sigmoid(x):
    return 0.5 * (1.0 + jnp.tanh(0.5 * x))


def _log_sigmoid(x):
    return jnp.minimum(x, 0.0) - jnp.log1p(jnp.exp(-jnp.abs(x)))


def _gate_transform(pre, is_input_gate):
    g = GATE_CAP * jnp.tanh(pre / GATE_CAP)
    return jnp.where(is_input_gate, g, _log_sigmoid(g))


def _a_inproj_kernel(x_ref, g_ref, w_ref, wst_ref, br_ref, q_ref, kt_ref, v_ref, o_ref, gc_ref, gr_ref):
    L = LANES
    xn = _rms(x_ref[...], g_ref[...])
    xh, xl = _split2(xn)
    r_all = _dot_nt(wst_ref[...], xh)
    kt_ref[0] = r_all[:QK_W].astype(BF16)
    pre = (r_all[QK_W:QK_W + GATE_ROWS] + r_all[QK_W + GATE_ROWS:]
           + _dot_nt(wst_ref[QK_W:QK_W + GATE_ROWS, :], xl) + br_ref[:, :1])
    sub = lax.broadcasted_iota(jnp.int32, pre.shape, 0)
    gates = _gate_transform(pre, sub < 8)
    r_i = lax.broadcasted_iota(jnp.int32, (L, L), 0)
    c_i = lax.broadcasted_iota(jnp.int32, (L, L), 1)
    causal = c_i <= r_i
    eye = c_i == r_i
    triu = (r_i <= c_i).astype(BF16)
    for c in range(x_ref.shape[0] // L):
        sl = slice(c * L, (c + 1) * L)
        gch = gates[:, sl]
        r1, r2, r3 = _split3(gch)
        cum = _dot(r1, triu) + _dot(r2, triu) + _dot(r3, triu)
        b_r = cum[8:16]
        a_r = gch[0:8] - b_r
        gr_ref[0, :, sl] = a_r
        out = jnp.zeros((L, L), F32)
        for h in range(N_HEADS):
            a_b = jnp.broadcast_to(a_r[h:h + 1, :], (L, L))
            b_b = jnp.broadcast_to(b_r[h:h + 1, :], (L, L))
            cm_col = jnp.max(jnp.where(causal, a_b, -jnp.inf), axis=1, keepdims=True)
            a_col = jnp.sum(jnp.where(eye, a_b, 0.0), axis=1, keepdims=True)
            b_col = jnp.sum(jnp.where(eye, b_b, 0.0), axis=1, keepdims=True)
            out = jnp.where(c_i == N_HEADS + h, b_col, out)
            out = jnp.where(c_i == 2 * N_HEADS + h, cm_col, out)
            out = jnp.where(c_i == 3 * N_HEADS + h, a_col, out)
        gc_ref[sl, :] = out
    p = _dot(xh, w_ref[...])
    q_ref[...] = (p[:, :QK_W] * (DK ** -0.5)).astype(BF16)
    v_ref[...] = p[:, QK_W:QK_W + V_W].astype(BF16)
    o_ref[...] = p[:, QK_W + V_W:]


def _a_inproj(x, g, w, wst, br, tm, seq):
    m, d = x.shape
    n_p = w.shape[1]
    tpb = seq // tm
    row = lambda i: (i, 0)
    colb = lambda i: (i // tpb, 0, i % tpb)
    return pl.pallas_call(
        _a_inproj_kernel,
        grid=(m // tm,),
        in_specs=[pl.BlockSpec((tm, d), row), _resident((1, d)), _resident((d, n_p)),
                  _resident(wst.shape), _resident((GATE_ROWS, LANES))],
        out_specs=[pl.BlockSpec((tm, QK_W), row), pl.BlockSpec((1, QK_W, tm), colb),
                   pl.BlockSpec((tm, V_W), row), pl.BlockSpec((tm, V_W), row),
                   pl.BlockSpec((tm, LANES), row), pl.BlockSpec((1, 8, tm), colb)],
        out_shape=[jax.ShapeDtypeStruct((m, QK_W), BF16), jax.ShapeDtypeStruct((m // seq, QK_W, seq), BF16),
                   jax.ShapeDtypeStruct((m, V_W), BF16), jax.ShapeDtypeStruct((m, V_W), F32),
                   jax.ShapeDtypeStruct((m, LANES), F32), jax.ShapeDtypeStruct((m // seq, 8, seq), F32)],
        compiler_params=_params(("arbitrary",)),
        name="a_inproj",
    )(x, g, w, wst, br)


def _mlstm_kernel(q_ref, kt_ref, v_ref, gc_ref, gr_ref, c0_ref, n0_ref, m0_ref,
                  h_ref, c_ref, n_ref, m_ref, ct_s, n_s, m_s, *, valid):
    L = LANES
    nb = q_ref.shape[0]
    step = pl.program_id(1)

    @pl.when(step == 0)
    def _load_state():
        for bb in range(nb):
            for h in range(N_HEADS):
                si = bb * N_HEADS + h
                ct_s[si] = c0_ref[bb, h].T
                n_s[si] = jnp.broadcast_to(n0_ref[bb, h:h + 1, :], (L, DK)).T
                m_s[si] = jnp.broadcast_to(m0_ref[bb, h:h + 1, :], (8, LANES))

    row = lax.broadcasted_iota(jnp.int32, (L, L), 0)
    col = lax.broadcasted_iota(jnp.int32, (L, L), 1)
    causal = col <= row
    ones = jnp.ones((L, LANES), BF16)

    for ck in range(q_ref.shape[1] // L):
        rows = slice(ck * L, (ck + 1) * L)
        chains = [(bb, h) for bb in range(nb) for h in range(N_HEADS)]
        st = {}
        for bb, h in chains:
            si = bb * N_HEADS + h
            gc = gc_ref[bb, rows, :]
            gr = gr_ref[bb, :, rows]
            b_rep = jnp.broadcast_to(gc[:, N_HEADS + h:N_HEADS + h + 1], (L, LANES))
            cm_rep = jnp.broadcast_to(gc[:, 2 * N_HEADS + h:2 * N_HEADS + h + 1], (L, LANES))
            a_rep = jnp.broadcast_to(gc[:, 3 * N_HEADS + h:3 * N_HEADS + h + 1], (L, LANES))
            a_row = gr[h:h + 1, :]
            m_prev = m_s[si, 0:1, :]
            qh = q_ref[bb, rows, h * DK:(h + 1) * DK]
            kth = kt_ref[bb, h * DK:(h + 1) * DK, rows]
            mx = jnp.maximum(m_prev, cm_rep)
            s = _dot(qh, kth) * jnp.where(causal, jnp.exp(a_row - mx), 0.0)
            st[si] = dict(b_rep=b_rep, a_rep=a_rep, m_prev=m_prev, mx=mx, sb=s.astype(BF16),
                          w_inter=jnp.exp(m_prev - mx), qh=qh, kth=kth)
        for bb, h in chains:
            si = bb * N_HEADS + h
            c = st[si]
            ct = ct_s[si]
            n_rep = n_s[si]
            v0 = v_ref[bb, rows, h * DV:h * DV + LANES]
            v1 = v_ref[bb, rows, h * DV + LANES:(h + 1) * DV]
            ctb = ct.astype(BF16)
            den = _dot(c["sb"], ones) + c["w_inter"] * _dot(c["qh"], n_rep.astype(BF16))
            m_t = c["b_rep"] + c["mx"]
            r = 1.0 / jnp.maximum(jnp.abs(den), jnp.exp(-m_t))
            h_ref[bb, rows, h * DV:h * DV + LANES] = (
                _dot(c["sb"], v0) + c["w_inter"] * _dot(c["qh"], ctb[:, :LANES])) * r
            h_ref[bb, rows, h * DV + LANES:(h + 1) * DV] = (
                _dot(c["sb"], v1) + c["w_inter"] * _dot(c["qh"], ctb[:, LANES:])) * r
            c.update(m_t=m_t, ct=ct, n_rep=n_rep, v0=v0, v1=v1)
        for bb, h in chains:
            si = bb * N_HEADS + h
            c = st[si]
            m_new = c["m_t"][valid - 1:valid, :]
            b_last = c["b_rep"][valid - 1:valid, :]
            decay = jnp.exp(b_last + c["m_prev"] - m_new)
            w_s = jnp.exp(c["a_rep"] + (b_last - m_new))
            if valid < L:
                w_s = jnp.where(row < valid, w_s, 0.0)
            wv = jnp.concatenate(
                [(w_s * c["v0"].astype(F32)).astype(BF16), (w_s * c["v1"].astype(F32)).astype(BF16)], axis=1)
            ct_s[si] = jnp.concatenate([decay, decay], axis=1) * c["ct"] + _dot(c["kth"], wv)
            n_s[si] = decay * c["n_rep"] + _dot(c["kth"], w_s.astype(BF16))
            m_s[si] = jnp.broadcast_to(m_new, (8, LANES))

    @pl.when(step == pl.num_programs(1) - 1)
    def _store_state():
        for bb in range(nb):
            for h in range(N_HEADS):
                si = bb * N_HEADS + h
                c_ref[bb, h] = ct_s[si].T
                n_ref[bb, h:h + 1, :] = n_s[si].T[0:1, :]
                m_ref[bb, h:h + 1, :] = m_s[si, 0:1, :]
        n_ref[:, N_HEADS:, :] = jnp.zeros((nb, 8 - N_HEADS, LANES), F32)
        m_ref[:, N_HEADS:, :] = jnp.zeros((nb, 8 - N_HEADS, LANES), F32)


def _mlstm(q, kt, v, gc, gr, c0, n0, m0, valid, nb, cps):
    b, s, _ = q.shape
    tok = cps * LANES
    steps = s // tok
    seq = lambda bi, i: (bi, i, 0)
    seq_t = lambda bi, i: (bi, 0, i)
    per_b3 = lambda bi, i: (bi, 0, 0)
    per_b4 = lambda bi, i: (bi, 0, 0, 0)
    return pl.pallas_call(
        functools.partial(_mlstm_kernel, valid=valid),
        grid=(b // nb, steps),
        in_specs=[pl.BlockSpec((nb, tok, QK_W), seq), pl.BlockSpec((nb, QK_W, tok), seq_t),
                  pl.BlockSpec((nb, tok, V_W), seq),
                  pl.BlockSpec((nb, tok, LANES), seq), pl.BlockSpec((nb, 8, tok), seq_t),
                  pl.BlockSpec((nb, N_HEADS, DV, DK), per_b4), pl.BlockSpec((nb, 8, LANES), per_b3),
                  pl.BlockSpec((nb, 8, LANES), per_b3)],
        out_specs=[pl.BlockSpec((nb, tok, V_W), seq), pl.BlockSpec((nb, N_HEADS, DV, DK), per_b4),
                   pl.BlockSpec((nb, 8, LANES), per_b3), pl.BlockSpec((nb, 8, LANES), per_b3)],
        out_shape=[jax.ShapeDtypeStruct((b, s, V_W), F32),
                   jax.ShapeDtypeStruct((b, N_HEADS, DV, DK), F32),
                   jax.ShapeDtypeStruct((b, 8, LANES), F32), jax.ShapeDtypeStruct((b, 8, LANES), F32)],
        scratch_shapes=[pltpu.VMEM((nb * N_HEADS, DK, DV), F32), pltpu.VMEM((nb * N_HEADS, DK, LANES), F32),
                        pltpu.VMEM((nb * N_HEADS, 8, LANES), F32)],
        compiler_params=_params(("arbitrary", "arbitrary")),
        name="mlstm_scan",
    )(q, kt, v, gc, gr, c0, n0, m0)


def _ffn_kernel(*refs, has_pre, final_norm, n_chunks):
    refs = list(refs)
    x_ref = refs.pop(0)
    if has_pre:
        h_ref, o_ref, go_ref, wo_ref = refs[:4]
        refs = refs[4:]
    g_ref, wgu_ref, wd_ref = refs[:3]
    refs = refs[3:]
    if final_norm:
        gf_ref = refs.pop(0)
    out_ref, a_s = refs[:2]
    x = x_ref[...]
    if has_pre:
        y_s = refs[2]
        for hd in range(N_HEADS):
            cs = slice(hd * DV, (hd + 1) * DV)
            hh = h_ref[:, cs]
            hn = hh * lax.rsqrt(jnp.mean(hh * hh, axis=1, keepdims=True) + EPS)
            y_s[:, cs] = (hn * go_ref[:, cs] * _sigmoid(o_ref[:, cs])).astype(BF16)
        x = x + _dot(y_s[...], wo_ref[...])
    xb = _rms(x, g_ref[...]).astype(BF16)
    dff = wd_ref.shape[0]
    cw = dff // n_chunks
    for c in range(n_chunks):
        cs = slice(c * cw, (c + 1) * cw)
        gate = _dot(xb, wgu_ref[:, cs])
        up = _dot(xb, wgu_ref[:, dff + c * cw:dff + (c + 1) * cw])
        a_s[:, cs] = (gate * _sigmoid(gate) * up).astype(BF16)
    out = x + _dot(a_s[...], wd_ref[...])
    if final_norm:
        out = _rms(out, gf_ref[...])
    out_ref[...] = out


def _ffn(x, pre, g, wgu, wd, layer, g_final, tm, n_chunks):
    m, d = x.shape
    dff = wd.shape[1]
    row = lambda i: (i, 0)
    args, specs = [x], [pl.BlockSpec((tm, d), row)]
    scratch = [pltpu.VMEM((tm, dff), BF16)]
    if pre is not None:
        hh, o, gout, wo = pre
        args += [hh, o, gout, wo]
        specs += [pl.BlockSpec((tm, V_W), row), pl.BlockSpec((tm, V_W), row), _resident((1, V_W)),
                  _resident(wo.shape)]
        scratch.append(pltpu.VMEM((tm, V_W), BF16))
    args += [g, wgu, wd]
    specs += [_resident((1, d)), _layer_resident(wgu, layer), _layer_resident(wd, layer)]
    if g_final is not None:
        args.append(g_final)
        specs.append(_resident((1, d)))
    return pl.pallas_call(
        functools.partial(_ffn_kernel, has_pre=pre is not None, final_norm=g_final is not None,
                          n_chunks=n_chunks),
        grid=(m // tm,),
        in_specs=specs,
        out_specs=pl.BlockSpec((tm, d), row),
        out_shape=jax.ShapeDtypeStruct((m, d), F32),
        scratch_shapes=scratch,
        compiler_params=_params(("arbitrary",)),
        name="ffn",
    )(*args)


def _bmix_kernel(x_ref, g_ref, win_ref, gv_ref, mix_ref, bias_ref, wout_ref, *out_refs, chunk_len, emit_v,
                 sub):
    if emit_v:
        out_ref, v_ref, z_s = out_refs
    else:
        out_ref, z_s = out_refs
    L = LANES
    tm = x_ref.shape[0]
    inner = gv_ref.shape[1]
    dg = inner // MLP_GROUPS

    def gelu(t):
        return 0.5 * t * (1.0 + jnp.tanh(0.7978845608028654 * (t + 0.044715 * (t * t * t))))

    row = lax.broadcasted_iota(jnp.int32, (L, L), 0)
    col = lax.broadcasted_iota(jnp.int32, (L, L), 1)
    if chunk_len == L:
        mask = col <= row
    else:
        mask = (row // chunk_len == col // chunk_len) & (col % chunk_len <= row % chunk_len)
    mgs = [jnp.where(mask, mix_ref[gi], 0.0).astype(BF16) for gi in range(MLP_GROUPS)]

    for t0 in range(0, tm, sub):
        ts = slice(t0, t0 + sub)
        x = x_ref[ts, :]
        xb = _rms(x, g_ref[...]).astype(BF16)
        v = gelu(_dot(xb, win_ref[:, inner:]))
        u = gelu(_dot(xb, win_ref[:, :inner]))
        mean = jnp.mean(v, axis=-1, keepdims=True)
        var = jnp.mean(v * v, axis=-1, keepdims=True) - mean * mean
        v = (v - mean) * lax.rsqrt(var + EPS) * gv_ref[...]
        if emit_v:
            v_ref[ts, :] = v
        vb = v.astype(BF16)
        for gi in range(MLP_GROUPS):
            bias = bias_ref[:, gi:gi + 1]
            gs = slice(gi * dg, (gi + 1) * dg)
            for c in range(sub // L):
                rs = slice(c * L, (c + 1) * L)
                sv = _dot(mgs[gi], vb[rs, gs]) + bias
                z_s[t0 + c * L:t0 + (c + 1) * L, gs] = (u[rs, gs] * sv).astype(BF16)
        out_ref[ts, :] = x + _dot(z_s[ts, :], wout_ref[...])


def _bmix(x, g, win, gv, mix, bias, wout, layer, tm, chunk_len, emit_v, sub):
    m, d = x.shape
    inner = gv.shape[1]
    row = lambda i: (i, 0)
    out_specs = [pl.BlockSpec((tm, d), row)]
    out_shape = [jax.ShapeDtypeStruct((m, d), F32)]
    if emit_v:
        out_specs.append(pl.BlockSpec((tm, inner), row))
        out_shape.append(jax.ShapeDtypeStruct((m, inner), F32))
    return pl.pallas_call(
        functools.partial(_bmix_kernel, chunk_len=chunk_len, emit_v=emit_v, sub=min(sub, tm)),
        grid=(m // tm,),
        in_specs=[pl.BlockSpec((tm, d), row), _resident((1, d)), _layer_resident(win, layer),
                  _resident((1, inner)), _resident(mix.shape), _resident(bias.shape),
                  _layer_resident(wout, layer)],
        out_specs=out_specs,
        out_shape=out_shape,
        scratch_shapes=[pltpu.VMEM((tm, inner), BF16)],
        compiler_params=_params(("arbitrary",)),
        name="chunk_mlp",
    )(x, g, win, gv, mix, bias, wout)


def _gate_rows(t):
    z = jnp.zeros((8 - N_HEADS, t.shape[1]), t.dtype)
    return jnp.concatenate([t[:N_HEADS], z, t[N_HEADS:], z], axis=0)


def _prep_weights(g_mix, g_ffn, g_final, a_w_in, a_b_gate, a_g_out, a_w_out,
                  b_w_in, b_g_v, b_w_s, b_b_s, b_w_out, f_w_gu, f_w_down):
    depth = g_mix.shape[0]
    n_proj = 2 * QK_W + 2 * V_W
    w = dict(depth=depth, g_final=g_final.reshape(1, -1))
    w["g_mix"] = [g_mix[i].reshape(1, -1) for i in range(depth)]
    w["g_ffn"] = [g_ffn[i].reshape(1, -1) for i in range(depth)]
    w["f_wgu"] = f_w_gu.astype(BF16)
    w["f_wd"] = f_w_down.astype(BF16)
    w["b_win"] = b_w_in.astype(BF16)
    w["b_wout"] = b_w_out.astype(BF16)
    a = []
    for j in range(a_w_in.shape[0]):
        wk_t = a_w_in[j, :, QK_W:2 * QK_W].T.astype(BF16)
        wg_hi, wg_lo = _split2(_gate_rows(a_w_in[j, :, n_proj:].T))
        a.append(dict(
            w=jnp.concatenate([a_w_in[j, :, :QK_W], a_w_in[j, :, 2 * QK_W:n_proj]], axis=1).astype(BF16),
            wst=jnp.concatenate([wk_t, wg_hi, wg_lo], axis=0),
            br=jnp.broadcast_to(_gate_rows(a_b_gate[j][:, None]), (GATE_ROWS, LANES)),
            gout=a_g_out[j].reshape(1, -1),
            wout=a_w_out[j].astype(BF16)))
    w["a"] = a
    w["b"] = [dict(gv=b_g_v[j].reshape(1, -1), ws=b_w_s[j], bs=b_b_s[j]) for j in range(b_w_in.shape[0])]
    return w


def _pack_state_vec(t):
    b, h, n = t.shape
    t = jnp.broadcast_to(t, (b, h, LANES)) if n == 1 else t
    return jnp.pad(t, ((0, 0), (0, 8 - h), (0, 0)))


def _trunk(x, w, c0, n0, m0, *, tm, nb, ff_chunks_wide):
    bsz, s, d = x.shape
    m = bsz * s
    valid = min(s, LANES)
    mlp_len = min(MLP_CHUNK, s)
    reps = LANES // mlp_len
    xf = x.reshape(m, d)
    c_out, n_out, m_out, v_rows = [], [], [], []
    for i in range(w["depth"]):
        j = i // 2
        pre = None
        if i % 2 == 0:
            a = w["a"][j]
            if valid < LANES:
                xa = jnp.pad(xf.reshape(bsz, s, d), ((0, 0), (0, LANES - s), (0, 0))).reshape(bsz * LANES, d)
                sp = LANES
            else:
                xa, sp = xf, s
            q, kt, v, o, gc, gr = _a_inproj(xa, w["g_mix"][i], a["w"], a["wst"], a["br"], min(2 * tm, sp), sp)
            hh, c_new, n_new, m_new = _mlstm(
                q.reshape(bsz, sp, QK_W), kt, v.reshape(bsz, sp, V_W), gc.reshape(bsz, sp, LANES), gr,
                c0[:, j], _pack_state_vec(n0[:, j]), _pack_state_vec(m0[:, j][..., None]), valid, nb,
                2 if sp % (2 * LANES) == 0 else 1)
            c_out.append(c_new)
            n_out.append(n_new[:, :N_HEADS, :])
            m_out.append(m_new[:, :N_HEADS, 0])
            pre = (hh[:, :s].reshape(m, V_W), o.reshape(bsz, sp, V_W)[:, :s].reshape(m, V_W), a["gout"],
                   a["wout"])
        else:
            bw = w["b"][j]
            mix = jnp.tile(bw["ws"][:, :mlp_len, :mlp_len], (1, reps, reps))
            bias = jnp.pad(jnp.tile(bw["bs"][:, :mlp_len], (1, reps)).T, ((0, 0), (0, LANES - MLP_GROUPS)))
            emit_v = s < MLP_CHUNK
            res = _bmix(xf, w["g_mix"][i], w["b_win"], bw["gv"], mix, bias, w["b_wout"], j,
                        min(tm, m), mlp_len, emit_v, 512)
            xf = res[0]
            if emit_v:
                v_rows.append(res[1].reshape(bsz, s, -1))
        g_final = w["g_final"] if i == w["depth"] - 1 else None
        wide = pre is None
        xf = _ffn(xf, pre, w["g_ffn"][i], w["f_wgu"], w["f_wd"], i, g_final,
                  min(2 * tm if wide else tm, m), ff_chunks_wide if wide else 1)
    return (xf.reshape(bsz, s, d), jnp.stack(c_out, axis=1), jnp.stack(n_out, axis=1),
            jnp.stack(m_out, axis=1), v_rows)


def kernel(x_prompt, x_sample, state_C, state_n, state_m, g_mix, g_ffn, g_final, a_w_in, a_b_gate, a_g_out,
           a_w_out, b_w_in, b_g_v, b_w_s, b_b_s, b_w_out, f_w_gu, f_w_down):
    w = _prep_weights(g_mix, g_ffn, g_final, a_w_in, a_b_gate, a_g_out, a_w_out,
                      b_w_in, b_g_v, b_w_s, b_b_s, b_w_out, f_w_gu, f_w_down)
    bp = x_prompt.shape[0]
    n_a = state_C.shape[1]
    zc = jnp.zeros((bp, n_a, N_HEADS, DV, DK), F32)
    zn = jnp.zeros((bp, n_a, N_HEADS, DK), F32)
    zm = jnp.zeros((bp, n_a, N_HEADS), F32)
    y_p, c_p, n_p, m_p, _ = _trunk(x_prompt, w, zc, zn, zm, tm=512, nb=4, ff_chunks_wide=11)
    y_s, c_s, n_s, m_s, v_list = _trunk(x_sample, w, state_C.astype(F32), state_n.astype(F32),
                                        state_m.astype(F32), tm=512, nb=4, ff_chunks_wide=11)
    v_rows = jnp.stack(v_list, axis=1)
    return (y_p, y_s, c_p, n_p, m_p, c_s, n_s, m_s, v_rows)
```
